```python
import jax, jax.numpy as jnp
from jax import lax
import numpy as np

D_MODEL = 1024
BATCH = 8
SEQ = 2048
DEPTH = 2

MEM_LEN = 256
EPS = 1e-6
MLA_HEADS = 8
QK_NOPE = 64
QK_ROPE = 32
QK_HEAD = QK_NOPE + QK_ROPE
V_HEAD = 64
Q_LORA = 256
KV_LORA = 128
ROPE_THETA = 10000.0
Q_BLOCK = 128
CONV_WIDTH = 512
CONV_K = 3
SG_WIDTH = 512
SG_GROUPS = 4
SG_CHUNK = 128
MEM_HEADS = 4
MEM_HEAD_DIM = 128
N_BRANCH = 4
BRANCH_WIDTH = 512

IN_SIZES = [Q_LORA, KV_LORA, QK_ROPE, 3 * CONV_WIDTH, 2 * SG_WIDTH,
            MEM_HEADS * MEM_HEAD_DIM, N_BRANCH * BRANCH_WIDTH, N_BRANCH * D_MODEL]
IN_WIDTH = sum(IN_SIZES)
IN_OFFSETS = [int(o) for o in np.cumsum(IN_SIZES)[:-1]]
NEG_INF = -1e30

kernel_name = "hybrid_gated_mla_conv_sgmlp_memory"


def rms_norm(x, g):
    x32 = x.astype(jnp.float32)
    y = x32 * lax.rsqrt(jnp.mean(x32 * x32, axis=-1, keepdims=True) + EPS)
    return (y * g.astype(jnp.float32)).astype(x.dtype)


def layer_norm(x, g, b):
    x32 = x.astype(jnp.float32)
    mu = jnp.mean(x32, axis=-1, keepdims=True)
    xc = x32 - mu
    y = xc * lax.rsqrt(jnp.mean(xc * xc, axis=-1, keepdims=True) + EPS)
    return (y * g.astype(jnp.float32) + b.astype(jnp.float32)).astype(x.dtype)


def apply_rope(x, pos):
    half = x.shape[-1] // 2
    inv_freq = ROPE_THETA ** (-jnp.arange(half, dtype=jnp.float32) / half)
    ang = pos.astype(jnp.float32)[..., None] * inv_freq
    cos = jnp.cos(ang)[:, :, None, :].astype(x.dtype)
    sin = jnp.sin(ang)[:, :, None, :].astype(x.dtype)
    x1, x2 = x[..., :half], x[..., half:]
    return jnp.concatenate([x1 * cos - x2 * sin, x1 * sin + x2 * cos], axis=-1)


def causal_block_attention(q, k, v, scale):
    S = q.shape[1]
    outs = []
    for i in range(S // Q_BLOCK):
        lo, hi = i * Q_BLOCK, (i + 1) * Q_BLOCK
        qb, kb, vb = q[:, lo:hi], k[:, :hi], v[:, :hi]
        s = jnp.einsum('bqhd,bkhd->bhqk', qb, kb).astype(jnp.float32) * scale
        mask = jnp.arange(hi)[None, :] <= (lo + jnp.arange(Q_BLOCK))[:, None]
        s = jnp.where(mask, s, NEG_INF)
        p = jax.nn.softmax(s, axis=-1).astype(vb.dtype)
        outs.append(jnp.einsum('bhqk,bkhd->bqhd', p, vb))
    return jnp.concatenate(outs, axis=1)


def mla_branch(c_q, c_kv, k_rope, pos, cq_g, ckv_g, w_uq, w_ukv, q_g, k_g):
    B, S, _ = c_q.shape
    q = (rms_norm(c_q, cq_g) @ w_uq).reshape(B, S, MLA_HEADS, QK_HEAD)
    kv = (rms_norm(c_kv, ckv_g) @ w_ukv).reshape(B, S, MLA_HEADS, QK_NOPE + V_HEAD)
    k_nope, v = kv[..., :QK_NOPE], kv[..., QK_NOPE:]
    k_r = jnp.broadcast_to(k_rope[:, :, None, :], (B, S, MLA_HEADS, QK_ROPE))
    k = jnp.concatenate([k_nope, k_r], axis=-1)
    q = rms_norm(q, q_g)
    k = rms_norm(k, k_g)
    q = jnp.concatenate([q[..., :QK_NOPE], apply_rope(q[..., QK_NOPE:], pos)], axis=-1)
    k = jnp.concatenate([k[..., :QK_NOPE], apply_rope(k[..., QK_NOPE:], pos)], axis=-1)
    o = causal_block_attention(q, k, v, QK_HEAD ** -0.5)
    return o.reshape(B, S, MLA_HEADS * V_HEAD)


def shortconv_branch(conv_in, conv_w, conv_b):
    b_gate, c_gate, xin = jnp.split(conv_in, 3, axis=-1)
    S = xin.shape[1]
    z = c_gate * xin
    zp = jnp.pad(z, ((0, 0), (CONV_K - 1, 0), (0, 0)))
    y = conv_b + conv_w[0] * zp[:, 0:S]
    for j in range(1, CONV_K):
        y = y + conv_w[j] * zp[:, j:j + S]
    return b_gate * y


def spatial_gating_branch(sg_in, ln_g, ln_b, w_s, b_s):
    u, v = jnp.split(sg_in, 2, axis=-1)
    B, S, _ = v.shape
    v = layer_norm(v, ln_g, ln_b)
    v = v.reshape(B, S // SG_CHUNK, SG_CHUNK, SG_GROUPS, SG_WIDTH // SG_GROUPS)
    tril = jnp.tril(jnp.ones((SG_CHUNK, SG_CHUNK), dtype=bool))
    w = jnp.where(tril[None], w_s, jnp.zeros((), w_s.dtype))
    mixed = jnp.einsum('gts,bcsgd->bctgd', w, v) + b_s.T[None, None, :, :, None]
    return u * mixed.reshape(B, S, SG_WIDTH)


def memory_branch(q_in, mem, mem_g, w_mem_kv, q_g, k_g):
    B, S, _ = q_in.shape
    M = mem.shape[1]
    q = rms_norm(q_in.reshape(B, S, MEM_HEADS, MEM_HEAD_DIM), q_g)
    kv = (rms_norm(mem, mem_g) @ w_mem_kv).reshape(B, M, 2, MEM_HEADS, MEM_HEAD_DIM)
    k = rms_norm(kv[:, :, 0], k_g)
    v = kv[:, :, 1]
    s = jnp.einsum('bqhd,bmhd->bhqm', q, k).astype(jnp.float32) * (MEM_HEAD_DIM ** -0.5)
    p = jax.nn.softmax(s, axis=-1).astype(v.dtype)
    o = jnp.einsum('bhqm,bmhd->bqhd', p, v)
    return o.reshape(B, S, MEM_HEADS * MEM_HEAD_DIM)


def hybrid_layer(x, mem, pos, norm_g, w_in, cq_g, ckv_g, w_uq, w_ukv, mla_qg, mla_kg,
                 conv_w, conv_b, sg_ln_g, sg_ln_b, w_s, b_s, mem_g, w_mem_kv, mem_qg, mem_kg,
                 b_merge, w_branch, w_out):
    B, S, D = x.shape
    h = rms_norm(x, norm_g)
    proj = h @ w_in
    c_q, c_kv, k_rope, conv_in, sg_in, mem_q, silu_gates, merge_logits = jnp.split(proj, IN_OFFSETS, axis=-1)
    y_a = mla_branch(c_q, c_kv, k_rope, pos, cq_g, ckv_g, w_uq, w_ukv, mla_qg, mla_kg)
    y_b = shortconv_branch(conv_in, conv_w, conv_b)
    y_c = spatial_gating_branch(sg_in, sg_ln_g, sg_ln_b, w_s, b_s)
    y_d = memory_branch(mem_q, mem, mem_g, w_mem_kv, mem_qg, mem_kg)
    ys = jnp.stack([y_a, y_b, y_c, y_d], axis=2)
    ys = ys * jax.nn.silu(silu_gates.reshape(B, S, N_BRANCH, BRANCH_WIDTH))
    z = jnp.einsum('bsnw,nwd->bsnd', ys, w_branch)
    gate = jax.nn.sigmoid(merge_logits.reshape(B, S, N_BRANCH, D) + b_merge)
    merged = jnp.sum(gate * z, axis=2)
    return x + merged @ w_out


def setup_inputs(seed: int = 0) -> dict:
    key = jax.random.key(seed)
    ks = jax.random.split(key, 32)
    f32 = jnp.float32
    L = DEPTH

    def nrm(k, shape, scale):
        return jax.random.normal(k, shape, f32) * scale

    def gain(k, shape):
        return 1.0 + 0.05 * jax.random.normal(k, shape, f32)

    x = jax.random.normal(ks[0], (BATCH, SEQ, D_MODEL), f32)
    mem = jax.random.normal(ks[1], (BATCH, MEM_LEN, D_MODEL), f32)
    offset = jax.random.randint(ks[2], (BATCH, 1), 0, 1024, dtype=jnp.int32)
    positions = offset + jnp.arange(SEQ, dtype=jnp.int32)[None, :]
    return {
        "x": x,
        "mem": mem,
        "positions": positions,
        "norm_g": gain(ks[3], (L, D_MODEL)),
        "w_in": nrm(ks[4], (L, D_MODEL, IN_WIDTH), D_MODEL ** -0.5),
        "cq_norm_g": gain(ks[5], (L, Q_LORA)),
        "ckv_norm_g": gain(ks[6], (L, KV_LORA)),
        "w_uq": nrm(ks[7], (L, Q_LORA, MLA_HEADS * QK_HEAD), Q_LORA ** -0.5),
        "w_ukv": nrm(ks[8], (L, KV_LORA, MLA_HEADS * (QK_NOPE + V_HEAD)), KV_LORA ** -0.5),
        "mla_q_norm_g": gain(ks[9], (L, QK_HEAD)),
        "mla_k_norm_g": gain(ks[10], (L, QK_HEAD)),
        "conv_w": nrm(ks[11], (L, CONV_K, CONV_WIDTH), CONV_K ** -0.5),
        "conv_b": nrm(ks[12], (L, CONV_WIDTH), 0.02),
        "sg_ln_g": gain(ks[13], (L, SG_WIDTH)),
        "sg_ln_b": nrm(ks[14], (L, SG_WIDTH), 0.02),
        "w_spatial": nrm(ks[15], (L, SG_GROUPS, SG_CHUNK, SG_CHUNK), 0.5 * SG_CHUNK ** -0.5),
        "b_spatial": 1.0 + 0.1 * jax.random.normal(ks[16], (L, SG_GROUPS, SG_CHUNK), f32),
        "mem_norm_g": gain(ks[17], (L, D_MODEL)),
        "w_mem_kv": nrm(ks[18], (L, D_MODEL, 2 * MEM_HEADS * MEM_HEAD_DIM), D_MODEL ** -0.5),
        "mem_q_norm_g": gain(ks[19], (L, MEM_HEAD_DIM)),
        "mem_k_norm_g": gain(ks[20], (L, MEM_HEAD_DIM)),
        "b_merge": nrm(ks[21], (L, N_BRANCH, D_MODEL), 0.1),
        "w_branch": nrm(ks[22], (L, N_BRANCH, BRANCH_WIDTH, D_MODEL), BRANCH_WIDTH ** -0.5),
        "w_out": nrm(ks[23], (L, D_MODEL, D_MODEL), D_MODEL ** -0.5),
    }


def reference(x, mem, positions, norm_g, w_in, cq_norm_g, ckv_norm_g, w_uq, w_ukv,
              mla_q_norm_g, mla_k_norm_g, conv_w, conv_b, sg_ln_g, sg_ln_b, w_spatial,
              b_spatial, mem_norm_g, w_mem_kv, mem_q_norm_g, mem_k_norm_g, b_merge,
              w_branch, w_out):
    for l in range(DEPTH):
        x = hybrid_layer(x, mem, positions, norm_g[l], w_in[l], cq_norm_g[l], ckv_norm_g[l],
                         w_uq[l], w_ukv[l], mla_q_norm_g[l], mla_k_norm_g[l], conv_w[l], conv_b[l],
                         sg_ln_g[l], sg_ln_b[l], w_spatial[l], b_spatial[l], mem_norm_g[l],
                         w_mem_kv[l], mem_q_norm_g[l], mem_k_norm_g[l], b_merge[l],
                         w_branch[l], w_out[l])
    return x
```

```python
import functools

import jax
import jax.numpy as jnp
from jax import lax
from jax.experimental import pallas as pl
from jax.experimental.pallas import tpu as pltpu

D_MODEL = 1024
DEPTH = 2
EPS = 1e-6
MLA_HEADS = 8
QK_NOPE = 64
QK_ROPE = 32
QK_HEAD = QK_NOPE + QK_ROPE
V_HEAD = 64
Q_LORA = 256
KV_LORA = 128
ROPE_THETA = 10000.0
CONV_WIDTH = 512
SG_WIDTH = 512
SG_GROUPS = 4
SG_CHUNK = 128
MEM_HEADS = 4
MEM_HEAD_DIM = 128
N_BRANCH = 4
BRANCH_WIDTH = 512
NEG_INF = -1e30

LANES = 128
HALF_ROPE = QK_ROPE // 2
OFF_CONV = Q_LORA + KV_LORA + QK_ROPE
C_CONV = 0
C_SG = C_CONV + 3 * CONV_WIDTH
C_MEMQ = C_SG + 2 * SG_WIDTH
C_SILU = C_MEMQ + MEM_HEADS * MEM_HEAD_DIM
C_LOGIT = C_SILU + N_BRANCH * BRANCH_WIDTH
C_END = C_LOGIT + N_BRANCH * D_MODEL

TILE_QKV = 512
TILE_LAYER = 256
VMEM_LIMIT_BYTES = 56 * 1024 * 1024

BF16 = jnp.bfloat16
F32 = jnp.float32


def _dot(a, b):
    return jnp.dot(a, b, preferred_element_type=F32)


def _dot_nt(a, b):
    return lax.dot_general(a, b, (((1,), (1,)), ((), ())), preferred_element_type=F32)


def _rms(x, g):
    return x * lax.rsqrt(jnp.mean(x * x, axis=-1, keepdims=True) + EPS) * g


def _sigmoid(x):
    return 1.0 / (1.0 + jnp.exp(-x))


def _silu(x):
    return x * _sigmoid(x)


def _rope_table_kernel(pos_ref, invf_ref, cos_ref, sin_ref):
    pos = pos_ref[0].astype(F32)
    ang = pos * invf_ref[...]
    lane = lax.broadcasted_iota(jnp.int32, ang.shape, 1)
    c = jnp.cos(ang)
    s = jnp.sin(ang)
    cos_ref[0] = jnp.where(lane < QK_NOPE, 1.0, jnp.where(lane < QK_HEAD, c, 0.0))
    sin_ref[0] = jnp.where(lane < QK_NOPE, 0.0,
                           jnp.where(lane < QK_NOPE + HALF_ROPE, -s, jnp.where(lane < QK_HEAD, s, 0.0)))


def _rope_tables(positions):
    b, s = positions.shape
    half = jnp.arange(HALF_ROPE, dtype=F32)
    inv_freq = ROPE_THETA ** (-half / HALF_ROPE)
    invf = jnp.concatenate([jnp.zeros((QK_NOPE,), F32), inv_freq, inv_freq,
                            jnp.zeros((LANES - QK_HEAD,), F32)])[None, :]
    t = TILE_QKV
    return pl.pallas_call(
        _rope_table_kernel,
        grid=(b, s // t),
        in_specs=[pl.BlockSpec((1, t, 1), lambda i, j: (i, j, 0)),
                  pl.BlockSpec((1, LANES), lambda i, j: (0, 0))],
        out_specs=[pl.BlockSpec((1, t, LANES), lambda i, j: (i, j, 0))] * 2,
        out_shape=[jax.ShapeDtypeStruct((b, s, LANES), F32)] * 2,
        name="rope_tables",
    )(positions[:, :, None], invf)


def _head_norm_rope(slab, gain, cos, sin, scale):
    lane = lax.broadcasted_iota(jnp.int32, slab.shape, 1)
    ss = jnp.sum(jnp.where(lane < QK_HEAD, slab * slab, 0.0), axis=-1, keepdims=True)
    qn = slab * lax.rsqrt(ss / QK_HEAD + EPS) * gain
    partner = pltpu.roll(qn, LANES - QK_ROPE, axis=1)
    rot = qn * cos + partner * sin
    if scale != 1.0:
        rot = rot * scale
    return rot.astype(BF16)


def _qkv_kernel(x_ref, ng_ref, wa_ref, cqg_ref, ckvg_ref, wuq_ref, wukv_ref, gq_ref, gk_ref,
                cos_ref, sin_ref, q_ref, k_ref, v_ref):
    h = _rms(x_ref[0], ng_ref[...]).astype(BF16)
    pa = _dot(h, wa_ref[...])
    cq = _rms(pa[:, :Q_LORA], cqg_ref[...]).astype(BF16)
    ckv = _rms(pa[:, Q_LORA:Q_LORA + KV_LORA], ckvg_ref[...]).astype(BF16)
    kr = pa[:, Q_LORA + KV_LORA:]
    q_all = _dot(cq, wuq_ref[...])
    kv_all = _dot(ckv, wukv_ref[...])
    cos = cos_ref[0]
    sin = sin_ref[0]
    for hd in range(MLA_HEADS):
        sl = slice(hd * LANES, (hd + 1) * LANES)
        q_ref[0, hd] = _head_norm_rope(q_all[:, sl], gq_ref[...], cos, sin, QK_HEAD ** -0.5)
        k_ref[0, hd] = _head_norm_rope(kv_all[:, sl] + kr, gk_ref[...], cos, sin, 1.0)
    for j in range(MLA_HEADS // 2):
        v_ref[0, j] = kv_all[:, MLA_HEADS * LANES + j * LANES:MLA_HEADS * LANES + (j + 1) * LANES].astype(BF16)


def _qkv(x, ng, wa, cqg, ckvg, wuq, wukv, gq, gk, cos, sin):
    b, s, d = x.shape
    t = TILE_QKV
    const = lambda shape: pl.BlockSpec(shape, lambda i, j: (0,) * len(shape))
    return pl.pallas_call(
        _qkv_kernel,
        grid=(b, s // t),
        in_specs=[pl.BlockSpec((1, t, d), lambda i, j: (i, j, 0)),
                  const(ng.shape), const(wa.shape), const(cqg.shape), const(ckvg.shape),
                  const(wuq.shape), const(wukv.shape), const(gq.shape), const(gk.shape),
                  pl.BlockSpec((1, t, LANES), lambda i, j: (i, j, 0)),
                  pl.BlockSpec((1, t, LANES), lambda i, j: (i, j, 0))],
        out_specs=[pl.BlockSpec((1, MLA_HEADS, t, LANES), lambda i, j: (i, 0, j, 0)),
                   pl.BlockSpec((1, MLA_HEADS, t, LANES), lambda i, j: (i, 0, j, 0)),
                   pl.BlockSpec((1, MLA_HEADS // 2, t, LANES), lambda i, j: (i, 0, j, 0))],
        out_shape=[jax.ShapeDtypeStruct((b, MLA_HEADS, s, LANES), BF16),
                   jax.ShapeDtypeStruct((b, MLA_HEADS, s, LANES), BF16),
                   jax.ShapeDtypeStruct((b, MLA_HEADS // 2, s, LANES), BF16)],
        compiler_params=pltpu.CompilerParams(dimension_semantics=("arbitrary", "arbitrary"),
                                             vmem_limit_bytes=VMEM_LIMIT_BYTES),
        name="mla_qkv",
    )(x, ng, wa, cqg, ckvg, wuq, wukv, gq, gk, cos, sin)


def _mem_kv_kernel(mem_ref, mg_ref, w_ref, kg_ref, k_ref, v_ref):
    mn = _rms(mem_ref[0], mg_ref[...]).astype(BF16)
    kv = _dot(mn, w_ref[...])
    for hd in range(MEM_HEADS):
        sl = slice(hd * MEM_HEAD_DIM, (hd + 1) * MEM_HEAD_DIM)
        k_ref[0, hd] = _rms(kv[:, sl], kg_ref[...]).astype(BF16)
        off = MEM_HEADS * MEM_HEAD_DIM
        v_ref[0, hd] = kv[:, off + hd * MEM_HEAD_DIM:off + (hd + 1) * MEM_HEAD_DIM].astype(BF16)


def _mem_kv(mem, mg, w, kg):
    b, m, d = mem.shape
    const = lambda shape: pl.BlockSpec(shape, lambda i: (0,) * len(shape))
    return pl.pallas_call(
        _mem_kv_kernel,
        grid=(b,),
        in_specs=[pl.BlockSpec((1, m, d), lambda i: (i, 0, 0)), const(mg.shape), const(w.shape), const(kg.shape)],
        out_specs=[pl.BlockSpec((1, MEM_HEADS, m, MEM_HEAD_DIM), lambda i: (i, 0, 0, 0))] * 2,
        out_shape=[jax.ShapeDtypeStruct((b, MEM_HEADS, m, MEM_HEAD_DIM), BF16)] * 2,
        compiler_params=pltpu.CompilerParams(dimension_semantics=("arbitrary",),
                                             vmem_limit_bytes=VMEM_LIMIT_BYTES),
        name="mem_kv",
    )(mem, mg, w, kg)


def _layer_kernel(x_ref, ng_ref, wc_ref, q_ref, k_ref, v_ref, km_ref, vm_ref,
                  convw_ref, convb_ref, lng_ref, lnb_ref, ws_ref, bs_ref, mqg_ref,
                  bm_ref, wb_ref, wo_ref, out_ref,
                  h_ref, ysg_ref, ztail_ref, ya_ref, mix_ref, merged_ref):
    tm = TILE_LAYER
    i = pl.program_id(1)
    h_ref[...] = _rms(x_ref[0], ng_ref[...]).astype(BF16)

    def proj(col, width):
        return _dot(h_ref[...], wc_ref[:, col:col + width])

    tri = (lax.broadcasted_iota(jnp.int32, (tm, tm), 1) <= lax.broadcasted_iota(jnp.int32, (tm, tm), 0))

    def attend(qh, kb, vb, state, mask):
        m, l, acc = state
        s = _dot_nt(qh, kb)
        if mask:
            s = jnp.where(tri, s, NEG_INF)
        m_new = jnp.maximum(m, jnp.max(s, axis=-1, keepdims=True))
        alpha = jnp.exp(m - m_new)
        p = jnp.exp(s - m_new)
        l = alpha * l + jnp.sum(p, axis=-1, keepdims=True)
        acc = alpha * acc + _dot(p.astype(BF16), vb)
        return m_new, l, acc

    def pair_body(j, carry):
        q0 = q_ref[0, 2 * j]
        q1 = q_ref[0, 2 * j + 1]

        def kv_block(kb):
            rows = pl.ds(pl.multiple_of(kb * tm, tm), tm)
            return k_ref[0, 2 * j, rows, :], k_ref[0, 2 * j + 1, rows, :], v_ref[0, j, rows, :]

        def kb_body(kb, st):
            k0, k1, vb = kv_block(kb)
            return attend(q0, k0, vb, st[0], False), attend(q1, k1, vb, st[1], False)

        init = (jnp.full((tm, 1), NEG_INF, F32), jnp.zeros((tm, 1), F32), jnp.zeros((tm, LANES), F32))
        st = lax.fori_loop(0, i, kb_body, (init, init))
        k0, k1, vb = kv_block(i)
        (_, l0, a0) = attend(q0, k0, vb, st[0], True)
        (_, l1, a1) = attend(q1, k1, vb, st[1], True)
        lane = lax.broadcasted_iota(jnp.int32, (tm, LANES), 1)
        ya_ref[j] = jnp.where(lane < V_HEAD, a0 * (1.0 / l0), a1 * (1.0 / l1))
        return carry

    lax.fori_loop(0, MLA_HEADS // 2, pair_body, 0)
    ya = jnp.concatenate([ya_ref[j] for j in range(MLA_HEADS // 2)], axis=1)
    ysg_ref[0] = (ya * _silu(proj(C_SILU, BRANCH_WIDTH))).astype(BF16)

    @pl.when(i == 0)
    def _():
        ztail_ref[...] = jnp.zeros_like(ztail_ref)

    half = CONV_WIDTH // 2
    for c in range(2):
        cs = slice(c * half, (c + 1) * half)
        bg = proj(C_CONV + c * half, half)
        cg = proj(C_CONV + CONV_WIDTH + c * half, half)
        xin = proj(C_CONV + 2 * CONV_WIDTH + c * half, half)
        z = cg * xin
        tail = ztail_ref[:, cs]
        row = lax.broadcasted_iota(jnp.int32, z.shape, 0)
        z1 = jnp.where(row == 0, tail[7:8, :], pltpu.roll(z, 1, axis=0))
        z2 = jnp.where(row == 0, tail[6:7, :], jnp.where(row == 1, tail[7:8, :], pltpu.roll(z, 2, axis=0)))
        y = convb_ref[:, cs] + convw_ref[0:1, cs] * z2
        y = y + convw_ref[1:2, cs] * z1
        y = y + convw_ref[2:3, cs] * z
        ztail_ref[:, cs] = z[tm - 8:tm, :]
        gate = proj(C_SILU + BRANCH_WIDTH + c * half, half)
        ysg_ref[1, :, cs] = (bg * y * _silu(gate)).astype(BF16)

    v = proj(C_SG + SG_WIDTH, SG_WIDTH)
    mu = jnp.mean(v, axis=-1, keepdims=True)
    vc = v - mu
    vn = (vc * lax.rsqrt(jnp.mean(vc * vc, axis=-1, keepdims=True) + EPS) * lng_ref[...] + lnb_ref[...]).astype(BF16)
    low = (lax.broadcasted_iota(jnp.int32, (SG_CHUNK, SG_CHUNK), 1)
           <= lax.broadcasted_iota(jnp.int32, (SG_CHUNK, SG_CHUNK), 0))
    gw = SG_WIDTH // SG_GROUPS
    for g in range(SG_GROUPS):
        wt = jnp.where(low, ws_ref[g], 0.0).astype(BF16)
        for c in range(tm // SG_CHUNK):
            rs = slice(c * SG_CHUNK, (c + 1) * SG_CHUNK)
            gs = slice(g * gw, (g + 1) * gw)
            mix_ref[rs, gs] = _dot(wt, vn[rs, gs]) + bs_ref[:, gs]
    u = proj(C_SG, SG_WIDTH)
    ysg_ref[2] = (u * mix_ref[...] * _silu(proj(C_SILU + 2 * BRANCH_WIDTH, BRANCH_WIDTH))).astype(BF16)

    mq = proj(C_MEMQ, MEM_HEADS * MEM_HEAD_DIM)
    outs = []
    for hd in range(MEM_HEADS):
        qh = (_rms(mq[:, hd * MEM_HEAD_DIM:(hd + 1) * MEM_HEAD_DIM], mqg_ref[...])
              * (MEM_HEAD_DIM ** -0.5)).astype(BF16)
        s = _dot_nt(qh, km_ref[0, hd])
        p = jnp.exp(s - jnp.max(s, axis=-1, keepdims=True))
        l = jnp.sum(p, axis=-1, keepdims=True)
        outs.append(_dot(p.astype(BF16), vm_ref[0, hd]) * (1.0 / l))
    yd = jnp.concatenate(outs, axis=1)
    ysg_ref[3] = (yd * _silu(proj(C_SILU + 3 * BRANCH_WIDTH, BRANCH_WIDTH))).astype(BF16)

    nc = D_MODEL // 2
    for c in range(2):
        acc = None
        for n in range(N_BRANCH):
            zc = _dot(ysg_ref[n], wb_ref[n, :, c * nc:(c + 1) * nc])
            gate = _sigmoid(proj(C_LOGIT + n * D_MODEL + c * nc, nc) + bm_ref[n:n + 1, c * nc:(c + 1) * nc])
            acc = gate * zc if acc is None else acc + gate * zc
        merged_ref[:, c * nc:(c + 1) * nc] = acc.astype(BF16)
    out_ref[0] = x_ref[0] + _dot(merged_ref[...], wo_ref[...])


def _layer(x, ng, wc, q, k, v, km, vm, convw, convb, lng, lnb, ws, bs, mqg, bm, wb, wo):
    b, s, d = x.shape
    tm = TILE_LAYER

    def const(shape):
        return pl.BlockSpec(shape, lambda i, j: (0,) * len(shape), pipeline_mode=pl.Buffered(1))

    def per_batch(shape):
        return pl.BlockSpec((1,) + shape[1:], lambda i, j: (i,) + (0,) * (len(shape) - 1),
                            pipeline_mode=pl.Buffered(1))

    return pl.pallas_call(
        _layer_kernel,
        grid=(b, s // tm),
        in_specs=[pl.BlockSpec((1, tm, d), lambda i, j: (i, j, 0)),
                  const(ng.shape), const(wc.shape),
                  pl.BlockSpec((1, MLA_HEADS, tm, LANES), lambda i, j: (i, 0, j, 0)),
                  per_batch(k.shape), per_batch(v.shape), per_batch(km.shape), per_batch(vm.shape),
                  const(convw.shape), const(convb.shape), const(lng.shape), const(lnb.shape),
                  const(ws.shape), const(bs.shape), const(mqg.shape), const(bm.shape),
                  const(wb.shape), const(wo.shape)],
        out_specs=pl.BlockSpec((1, tm, d), lambda i, j: (i, j, 0)),
        out_shape=jax.ShapeDtypeStruct((b, s, d), F32),
        scratch_shapes=[pltpu.VMEM((tm, d), BF16),
                        pltpu.VMEM((N_BRANCH, tm, BRANCH_WIDTH), BF16),
                        pltpu.VMEM((8, CONV_WIDTH), F32),
                        pltpu.VMEM((MLA_HEADS // 2, tm, LANES), F32),
                        pltpu.VMEM((tm, SG_WIDTH), F32),
                        pltpu.VMEM((tm, d), BF16)],
        compiler_params=pltpu.CompilerParams(dimension_semantics=("arbitrary", "arbitrary"),
                                             vmem_limit_bytes=VMEM_LIMIT_BYTES),
        name="hybrid_layer",
    )(x, ng, wc, q, k, v, km, vm, convw, convb, lng, lnb, ws, bs, mqg, bm, wb, wo)


def _rope_slab_cols(r):
    r1, r2 = r[..., :HALF_ROPE], r[..., HALF_ROPE:]
    return jnp.concatenate([r1, r2, r2, r1], axis=-1)


def _head_gain_slab(g):
    return jnp.concatenate([g[:QK_NOPE], _rope_slab_cols(g[QK_NOPE:])])[None, :]


def _prep_layer(w_in, w_uq, w_ukv, q_g, k_g, b_spatial):
    kr = w_in[:, Q_LORA + KV_LORA:OFF_CONV]
    wa = jnp.concatenate([w_in[:, :Q_LORA + KV_LORA], jnp.zeros((D_MODEL, QK_NOPE), F32),
                          _rope_slab_cols(kr)], axis=1).astype(BF16)
    wc = w_in[:, OFF_CONV:].astype(BF16)
    uq = w_uq.reshape(Q_LORA, MLA_HEADS, QK_HEAD)
    wuq = jnp.concatenate([uq[..., :QK_NOPE], _rope_slab_cols(uq[..., QK_NOPE:])], axis=-1)
    wuq = wuq.reshape(Q_LORA, MLA_HEADS * LANES).astype(BF16)
    ukv = w_ukv.reshape(KV_LORA, MLA_HEADS, QK_NOPE + V_HEAD)
    wuk = jnp.concatenate([ukv[..., :QK_NOPE], jnp.zeros((KV_LORA, MLA_HEADS, LANES - QK_NOPE), F32)], axis=-1)
    wukv = jnp.concatenate([wuk.reshape(KV_LORA, MLA_HEADS * LANES),
                            ukv[..., QK_NOPE:].reshape(KV_LORA, MLA_HEADS * V_HEAD)], axis=1).astype(BF16)
    bs = jnp.repeat(b_spatial.T, SG_WIDTH // SG_GROUPS, axis=1)
    return wa, wc, wuq, wukv, _head_gain_slab(q_g), _head_gain_slab(k_g), bs


def kernel(x, mem, positions, norm_g, w_in, cq_norm_g, ckv_norm_g, w_uq, w_ukv, mla_q_norm_g, mla_k_norm_g,
           conv_w, conv_b, sg_ln_g, sg_ln_b, w_spatial, b_spatial, mem_norm_g, w_mem_kv, mem_q_norm_g,
           mem_k_norm_g, b_merge, w_branch, w_out):
    assert x.shape[1] % TILE_QKV == 0 and x.shape[2] == D_MODEL
    cos, sin = _rope_tables(positions)
    row = lambda a: a[None, :]
    for l in range(DEPTH):
        wa, wc, wuq, wukv, gq, gk, bs = _prep_layer(w_in[l], w_uq[l], w_ukv[l], mla_q_norm_g[l],
                                                    mla_k_norm_g[l], b_spatial[l])
        ng = row(norm_g[l])
        q, k, v = _qkv(x, ng, wa, row(cq_norm_g[l]), row(ckv_norm_g[l]), wuq, wukv, gq, gk, cos, sin)
        km, vm = _mem_kv(mem, row(mem_norm_g[l]), w_mem_kv[l].astype(BF16), row(mem_k_norm_g[l]))
        x = _layer(x, ng, wc, q, k, v, km, vm, conv_w[l], row(conv_b[l]), row(sg_ln_g[l]), row(sg_ln_b[l]),
                   w_spatial[l], bs, row(mem_q_norm_g[l]), b_merge[l], w_branch[l].astype(BF16),
                   w_out[l].astype(BF16))
    return x
```

```python
import functools

import jax
import jax.numpy as jnp
from jax import lax
from jax.experimental import pallas as pl
from jax.experimental.pallas import tpu as pltpu

D_MODEL = 1024
DEPTH = 2
EPS = 1e-6
MLA_HEADS = 8
QK_NOPE = 64
QK_ROPE = 32
QK_HEAD = QK_NOPE + QK_ROPE
V_HEAD = 64
Q_LORA = 256
KV_LORA = 128
ROPE_THETA = 10000.0
CONV_WIDTH = 512
SG_WIDTH = 512
SG_GROUPS = 4
SG_CHUNK = 128
MEM_HEADS = 4
MEM_HEAD_DIM = 128
N_BRANCH = 4
BRANCH_WIDTH = 512
NEG_INF = -1e30
LOG2E = 1.4426950408889634

LANES = 128
HALF_ROPE = QK_ROPE // 2
OFF_CONV = Q_LORA + KV_LORA + QK_ROPE
C_CONV = 0
C_SG = C_CONV + 3 * CONV_WIDTH
C_MEMQ = C_SG + 2 * SG_WIDTH
C_SILU = C_MEMQ + MEM_HEADS * MEM_HEAD_DIM
C_LOGIT = C_SILU + N_BRANCH * BRANCH_WIDTH
C_END = C_LOGIT + N_BRANCH * D_MODEL

TILE_QKV = 512
TILE_LAYER = 256
VMEM_LIMIT_BYTES = 56 * 1024 * 1024

BF16 = jnp.bfloat16
F32 = jnp.float32


def _dot(a, b):
    return jnp.dot(a, b, preferred_element_type=F32)


def _dot_nt(a, b):
    return lax.dot_general(a, b, (((1,), (1,)), ((), ())), preferred_element_type=F32)


def _rms(x, g):
    return x * lax.rsqrt(jnp.mean(x * x, axis=-1, keepdims=True) + EPS) * g


def _sigmoid(x):
    return 1.0 / (1.0 + jnp.exp(-x))


def _silu(x):
    return x * _sigmoid(x)


def _rope_table_kernel(pos_ref, invf_ref, cos_ref, sin_ref):
    pos = pos_ref[0].astype(F32)
    ang = pos * invf_ref[...]
    lane = lax.broadcasted_iota(jnp.int32, ang.shape, 1)
    c = jnp.cos(ang)
    s = jnp.sin(ang)
    cos_ref[0] = jnp.where(lane < QK_NOPE, 1.0, jnp.where(lane < QK_HEAD, c, 0.0))
    sin_ref[0] = jnp.where(lane < QK_NOPE, 0.0,
                           jnp.where(lane < QK_NOPE + HALF_ROPE, -s, jnp.where(lane < QK_HEAD, s, 0.0)))


def _rope_tables(positions):
    b, s = positions.shape
    half = jnp.arange(HALF_ROPE, dtype=F32)
    inv_freq = ROPE_THETA ** (-half / HALF_ROPE)
    invf = jnp.concatenate([jnp.zeros((QK_NOPE,), F32), inv_freq, inv_freq,
                            jnp.zeros((LANES - QK_HEAD,), F32)])[None, :]
    t = TILE_QKV
    return pl.pallas_call(
        _rope_table_kernel,
        grid=(b, s // t),
        in_specs=[pl.BlockSpec((1, t, 1), lambda i, j: (i, j, 0)),
                  pl.BlockSpec((1, LANES), lambda i, j: (0, 0))],
        out_specs=[pl.BlockSpec((1, t, LANES), lambda i, j: (i, j, 0))] * 2,
        out_shape=[jax.ShapeDtypeStruct((b, s, LANES), F32)] * 2,
        name="rope_tables",
    )(positions[:, :, None], invf)


def _head_norm_rope(slab, gain, cos, sin, scale):
    lane = lax.broadcasted_iota(jnp.int32, slab.shape, 1)
    ss = jnp.sum(jnp.where(lane < QK_HEAD, slab * slab, 0.0), axis=-1, keepdims=True)
    qn = slab * lax.rsqrt(ss / QK_HEAD + EPS) * gain
    partner = pltpu.roll(qn, LANES - QK_ROPE, axis=1)
    rot = qn * cos + partner * sin
    if scale != 1.0:
        rot = rot * scale
    return rot.astype(BF16)


def _qkv_kernel(x_ref, ng_ref, wa_ref, cqg_ref, ckvg_ref, wuq_ref, wuk_ref, wuvt_ref, gq_ref, gk_ref,
                cos_ref, sin_ref, q_ref, k_ref, vt_ref):
    h = _rms(x_ref[0], ng_ref[...]).astype(BF16)
    pa = _dot(h, wa_ref[...])
    cq = _rms(pa[:, :Q_LORA], cqg_ref[...]).astype(BF16)
    ckv = _rms(pa[:, Q_LORA:Q_LORA + KV_LORA], ckvg_ref[...]).astype(BF16)
    kr = pa[:, Q_LORA + KV_LORA:]
    q_all = _dot(cq, wuq_ref[...])
    k_all = _dot(ckv, wuk_ref[...])
    vt = _dot_nt(wuvt_ref[...], ckv).astype(BF16)
    cos = cos_ref[0]
    sin = sin_ref[0]
    for hd in range(MLA_HEADS):
        sl = slice(hd * LANES, (hd + 1) * LANES)
        q_ref[0, hd] = _head_norm_rope(q_all[:, sl], gq_ref[...], cos, sin, QK_HEAD ** -0.5 * LOG2E)
        k_ref[0, hd] = _head_norm_rope(k_all[:, sl] + kr, gk_ref[...], cos, sin, 1.0)
        for c in range(TILE_QKV // TILE_LAYER):
            vt_ref[0, hd, c] = vt[hd * V_HEAD:(hd + 1) * V_HEAD, c * TILE_LAYER:(c + 1) * TILE_LAYER]


def _qkv(x, ng, wa, cqg, ckvg, wuq, wuk, wuvt, gq, gk, cos, sin):
    b, s, d = x.shape
    t = TILE_QKV
    nkb = t // TILE_LAYER
    const = lambda shape: pl.BlockSpec(shape, lambda i, j: (0,) * len(shape))
    return pl.pallas_call(
        _qkv_kernel,
        grid=(b, s // t),
        in_specs=[pl.BlockSpec((1, t, d), lambda i, j: (i, j, 0)),
                  const(ng.shape), const(wa.shape), const(cqg.shape), const(ckvg.shape),
                  const(wuq.shape), const(wuk.shape), const(wuvt.shape), const(gq.shape), const(gk.shape),
                  pl.BlockSpec((1, t, LANES), lambda i, j: (i, j, 0)),
                  pl.BlockSpec((1, t, LANES), lambda i, j: (i, j, 0))],
        out_specs=[pl.BlockSpec((1, MLA_HEADS, t, LANES), lambda i, j: (i, 0, j, 0)),
                   pl.BlockSpec((1, MLA_HEADS, t, LANES), lambda i, j: (i, 0, j, 0)),
                   pl.BlockSpec((1, MLA_HEADS, nkb, V_HEAD, TILE_LAYER), lambda i, j: (i, 0, j, 0, 0))],
        out_shape=[jax.ShapeDtypeStruct((b, MLA_HEADS, s, LANES), BF16),
                   jax.ShapeDtypeStruct((b, MLA_HEADS, s, LANES), BF16),
                   jax.ShapeDtypeStruct((b, MLA_HEADS, s // TILE_LAYER, V_HEAD, TILE_LAYER), BF16)],
        compiler_params=pltpu.CompilerParams(dimension_semantics=("arbitrary", "arbitrary"),
                                             vmem_limit_bytes=VMEM_LIMIT_BYTES),
        name="mla_qkv",
    )(x, ng, wa, cqg, ckvg, wuq, wuk, wuvt, gq, gk, cos, sin)


def _mem_kv_kernel(mem_ref, mg_ref, w_ref, kg_ref, k_ref, v_ref):
    mn = _rms(mem_ref[0], mg_ref[...]).astype(BF16)
    kv = _dot(mn, w_ref[...])
    for hd in range(MEM_HEADS):
        sl = slice(hd * MEM_HEAD_DIM, (hd + 1) * MEM_HEAD_DIM)
        k_ref[0, hd] = _rms(kv[:, sl], kg_ref[...]).astype(BF16)
        off = MEM_HEADS * MEM_HEAD_DIM
        v_ref[0, hd] = kv[:, off + hd * MEM_HEAD_DIM:off + (hd + 1) * MEM_HEAD_DIM].astype(BF16)


def _mem_kv(mem, mg, w, kg):
    b, m, d = mem.shape
    const = lambda shape: pl.BlockSpec(shape, lambda i: (0,) * len(shape))
    return pl.pallas_call(
        _mem_kv_kernel,
        grid=(b,),
        in_specs=[pl.BlockSpec((1, m, d), lambda i: (i, 0, 0)), const(mg.shape), const(w.shape), const(kg.shape)],
        out_specs=[pl.BlockSpec((1, MEM_HEADS, m, MEM_HEAD_DIM), lambda i: (i, 0, 0, 0))] * 2,
        out_shape=[jax.ShapeDtypeStruct((b, MEM_HEADS, m, MEM_HEAD_DIM), BF16)] * 2,
        compiler_params=pltpu.CompilerParams(dimension_semantics=("arbitrary",),
                                             vmem_limit_bytes=VMEM_LIMIT_BYTES),
        name="mem_kv",
    )(mem, mg, w, kg)


def _layer_kernel(x_ref, ng_ref, wc_ref, q_ref, k_ref, vt_ref, km_ref, vm_ref,
                  convw_ref, convb_ref, lng_ref, lnb_ref, ws_ref, bs_ref, mqg_ref,
                  bm_ref, wb_ref, wo_ref, out_ref,
                  h_ref, ysg_ref, ztail_ref, m_ref, l_ref, acc_ref, mix_ref, merged_ref):
    tm = TILE_LAYER
    i = pl.program_id(1)
    h_ref[...] = _rms(x_ref[0], ng_ref[...]).astype(BF16)

    def proj(col, width):
        return _dot(h_ref[...], wc_ref[:, col:col + width])

    m_ref[...] = jnp.full(m_ref.shape, NEG_INF, F32)
    l_ref[...] = jnp.zeros(l_ref.shape, F32)
    acc_ref[...] = jnp.zeros(acc_ref.shape, F32)

    def attend(kb, mask):
        rows = pl.ds(pl.multiple_of(kb * tm, tm), tm)
        new = []
        ahead = 2
        scores = [_dot_nt(k_ref[0, hd, rows, :], q_ref[0, hd]) for hd in range(ahead)]
        for hd in range(MLA_HEADS):
            if hd + ahead < MLA_HEADS:
                scores.append(_dot_nt(k_ref[0, hd + ahead, rows, :], q_ref[0, hd + ahead]))
            st = scores[hd]
            if mask:
                key = lax.broadcasted_iota(jnp.int32, (tm, tm), 0)
                qry = lax.broadcasted_iota(jnp.int32, (tm, tm), 1)
                st = jnp.where(key <= qry, st, NEG_INF)
            m_old = m_ref[hd]
            m_new = jnp.maximum(m_old, jnp.max(st, axis=0, keepdims=True))
            alpha = jnp.exp2(m_old - m_new)
            p = jnp.exp2(st - m_new)
            new.append((m_new, alpha * l_ref[hd] + jnp.sum(p, axis=0, keepdims=True),
                        alpha * acc_ref[hd] + _dot(vt_ref[0, hd, kb], p.astype(BF16))))
        for hd, (m_new, l_new, acc_new) in enumerate(new):
            m_ref[hd] = m_new
            l_ref[hd] = l_new
            acc_ref[hd] = acc_new

    def kb_body(kb, carry):
        attend(kb, False)
        return carry

    lax.fori_loop(0, i, kb_body, 0)
    attend(i, True)
    yat = jnp.concatenate([acc_ref[hd] * (1.0 / l_ref[hd]) for hd in range(MLA_HEADS)], axis=0)
    ysg_ref[0] = (yat.T * _silu(proj(C_SILU, BRANCH_WIDTH))).astype(BF16)

    @pl.when(i == 0)
    def _():
        ztail_ref[...] = jnp.zeros_like(ztail_ref)

    half = CONV_WIDTH // 2
    for c in range(2):
        cs = slice(c * half, (c + 1) * half)
        bg = proj(C_CONV + c * half, half)
        cg = proj(C_CONV + CONV_WIDTH + c * half, half)
        xin = proj(C_CONV + 2 * CONV_WIDTH + c * half, half)
        z = cg * xin
        tail = ztail_ref[:, cs]
        row = lax.broadcasted_iota(jnp.int32, z.shape, 0)
        z1 = jnp.where(row == 0, tail[7:8, :], pltpu.roll(z, 1, axis=0))
        z2 = jnp.where(row == 0, tail[6:7, :], jnp.where(row == 1, tail[7:8, :], pltpu.roll(z, 2, axis=0)))
        y = convb_ref[:, cs] + convw_ref[0:1, cs] * z2
        y = y + convw_ref[1:2, cs] * z1
        y = y + convw_ref[2:3, cs] * z
        ztail_ref[:, cs] = z[tm - 8:tm, :]
        gate = proj(C_SILU + BRANCH_WIDTH + c * half, half)
        ysg_ref[1, :, cs] = (bg * y * _silu(gate)).astype(BF16)

    v = proj(C_SG + SG_WIDTH, SG_WIDTH)
    mu = jnp.mean(v, axis=-1, keepdims=True)
    vc = v - mu
    vn = (vc * lax.rsqrt(jnp.mean(vc * vc, axis=-1, keepdims=True) + EPS) * lng_ref[...] + lnb_ref[...]).astype(BF16)
    low = (lax.broadcasted_iota(jnp.int32, (SG_CHUNK, SG_CHUNK), 1)
           <= lax.broadcasted_iota(jnp.int32, (SG_CHUNK, SG_CHUNK), 0))
    gw = SG_WIDTH // SG_GROUPS
    for g in range(SG_GROUPS):
        wt = jnp.where(low, ws_ref[g], 0.0).astype(BF16)
        for c in range(tm // SG_CHUNK):
            rs = slice(c * SG_CHUNK, (c + 1) * SG_CHUNK)
            gs = slice(g * gw, (g + 1) * gw)
            mix_ref[rs, gs] = _dot(wt, vn[rs, gs]) + bs_ref[:, gs]
    u = proj(C_SG, SG_WIDTH)
    ysg_ref[2] = (u * mix_ref[...] * _silu(proj(C_SILU + 2 * BRANCH_WIDTH, BRANCH_WIDTH))).astype(BF16)

    mq = proj(C_MEMQ, MEM_HEADS * MEM_HEAD_DIM)
    outs = []
    for hd in range(MEM_HEADS):
        qh = (_rms(mq[:, hd * MEM_HEAD_DIM:(hd + 1) * MEM_HEAD_DIM], mqg_ref[...])
              * (MEM_HEAD_DIM ** -0.5)).astype(BF16)
        s = _dot_nt(qh, km_ref[0, hd])
        p = jnp.exp(s - jnp.max(s, axis=-1, keepdims=True))
        l = jnp.sum(p, axis=-1, keepdims=True)
        outs.append(_dot(p.astype(BF16), vm_ref[0, hd]) * (1.0 / l))
    yd = jnp.concatenate(outs, axis=1)
    ysg_ref[3] = (yd * _silu(proj(C_SILU + 3 * BRANCH_WIDTH, BRANCH_WIDTH))).astype(BF16)

    nc = D_MODEL // 2
    for c in range(2):
        acc = None
        for n in range(N_BRANCH):
            zc = _dot(ysg_ref[n], wb_ref[n, :, c * nc:(c + 1) * nc])
            gate = _sigmoid(proj(C_LOGIT + n * D_MODEL + c * nc, nc) + bm_ref[n:n + 1, c * nc:(c + 1) * nc])
            acc = gate * zc if acc is None else acc + gate * zc
        merged_ref[:, c * nc:(c + 1) * nc] = acc.astype(BF16)
    out_ref[0] = x_ref[0] + _dot(merged_ref[...], wo_ref[...])


def _layer(x, ng, wc, q, k, v, km, vm, convw, convb, lng, lnb, ws, bs, mqg, bm, wb, wo):
    b, s, d = x.shape
    tm = TILE_LAYER

    def const(shape):
        return pl.BlockSpec(shape, lambda i, j: (0,) * len(shape), pipeline_mode=pl.Buffered(1))

    def per_batch(shape):
        return pl.BlockSpec((1,) + shape[1:], lambda i, j: (i,) + (0,) * (len(shape) - 1),
                            pipeline_mode=pl.Buffered(1))

    return pl.pallas_call(
        _layer_kernel,
        grid=(b, s // tm),
        in_specs=[pl.BlockSpec((1, tm, d), lambda i, j: (i, j, 0)),
                  const(ng.shape), const(wc.shape),
                  pl.BlockSpec((1, MLA_HEADS, tm, LANES), lambda i, j: (i, 0, j, 0)),
                  per_batch(k.shape), per_batch(v.shape), per_batch(km.shape), per_batch(vm.shape),
                  const(convw.shape), const(convb.shape), const(lng.shape), const(lnb.shape),
                  const(ws.shape), const(bs.shape), const(mqg.shape), const(bm.shape),
                  const(wb.shape), const(wo.shape)],
        out_specs=pl.BlockSpec((1, tm, d), lambda i, j: (i, j, 0)),
        out_shape=jax.ShapeDtypeStruct((b, s, d), F32),
        scratch_shapes=[pltpu.VMEM((tm, d), BF16),
                        pltpu.VMEM((N_BRANCH, tm, BRANCH_WIDTH), BF16),
                        pltpu.VMEM((8, CONV_WIDTH), F32),
                        pltpu.VMEM((MLA_HEADS, 1, tm), F32),
                        pltpu.VMEM((MLA_HEADS, 1, tm), F32),
                        pltpu.VMEM((MLA_HEADS, V_HEAD, tm), F32),
                        pltpu.VMEM((tm, SG_WIDTH), F32),
                        pltpu.VMEM((tm, d), BF16)],
        compiler_params=pltpu.CompilerParams(dimension_semantics=("arbitrary", "arbitrary"),
                                             vmem_limit_bytes=VMEM_LIMIT_BYTES),
        name="hybrid_layer",
    )(x, ng, wc, q, k, v, km, vm, convw, convb, lng, lnb, ws, bs, mqg, bm, wb, wo)


def _rope_slab_cols(r):
    r1, r2 = r[..., :HALF_ROPE], r[..., HALF_ROPE:]
    return jnp.concatenate([r1, r2, r2, r1], axis=-1)


def _head_gain_slab(g):
    return jnp.concatenate([g[:QK_NOPE], _rope_slab_cols(g[QK_NOPE:])])[None, :]


def _prep_layer(w_in, w_uq, w_ukv, q_g, k_g, b_spatial):
    kr = w_in[:, Q_LORA + KV_LORA:OFF_CONV]
    wa = jnp.concatenate([w_in[:, :Q_LORA + KV_LORA], jnp.zeros((D_MODEL, QK_NOPE), F32),
                          _rope_slab_cols(kr)], axis=1).astype(BF16)
    wc = w_in[:, OFF_CONV:].astype(BF16)
    uq = w_uq.reshape(Q_LORA, MLA_HEADS, QK_HEAD)
    wuq = jnp.concatenate([uq[..., :QK_NOPE], _rope_slab_cols(uq[..., QK_NOPE:])], axis=-1)
    wuq = wuq.reshape(Q_LORA, MLA_HEADS * LANES).astype(BF16)
    ukv = w_ukv.reshape(KV_LORA, MLA_HEADS, QK_NOPE + V_HEAD)
    wuk = jnp.concatenate([ukv[..., :QK_NOPE], jnp.zeros((KV_LORA, MLA_HEADS, LANES - QK_NOPE), F32)], axis=-1)
    wuk = wuk.reshape(KV_LORA, MLA_HEADS * LANES).astype(BF16)
    wuvt = ukv[..., QK_NOPE:].reshape(KV_LORA, MLA_HEADS * V_HEAD).T.astype(BF16)
    bs = jnp.repeat(b_spatial.T, SG_WIDTH // SG_GROUPS, axis=1)
    return wa, wc, wuq, wuk, wuvt, _head_gain_slab(q_g), _head_gain_slab(k_g), bs


def kernel(x, mem, positions, norm_g, w_in, cq_norm_g, ckv_norm_g, w_uq, w_ukv, mla_q_norm_g, mla_k_norm_g,
           conv_w, conv_b, sg_ln_g, sg_ln_b, w_spatial, b_spatial, mem_norm_g, w_mem_kv, mem_q_norm_g,
           mem_k_norm_g, b_merge, w_branch, w_out):
    assert x.shape[1] % TILE_QKV == 0 and x.shape[2] == D_MODEL
    cos, sin = _rope_tables(positions)
    row = lambda a: a[None, :]
    for l in range(DEPTH):
        wa, wc, wuq, wuk, wuvt, gq, gk, bs = _prep_layer(w_in[l], w_uq[l], w_ukv[l], mla_q_norm_g[l],
                                                         mla_k_norm_g[l], b_spatial[l])
        ng = row(norm_g[l])
        q, k, v = _qkv(x, ng, wa, row(cq_norm_g[l]), row(ckv_norm_g[l]), wuq, wuk, wuvt, gq, gk, cos, sin)
        km, vm = _mem_kv(mem, row(mem_norm_g[l]), w_mem_kv[l].astype(BF16), row(mem_k_norm_g[l]))
        x = _layer(x, ng, wc, q, k, v, km, vm, conv_w[l], row(conv_b[l]), row(sg_ln_g[l]), row(sg_ln_b[l]),
                   w_spatial[l], bs, row(mem_q_norm_g[l]), b_merge[l], w_branch[l].astype(BF16),
                   w_out[l].astype(BF16))
    return x
```

```python
import functools

import jax
import jax.numpy as jnp
from jax import lax
from jax.experimental import pallas as pl
from jax.experimental.pallas import tpu as pltpu

D_MODEL = 1024
DEPTH = 2
EPS = 1e-6
MLA_HEADS = 8
QK_NOPE = 64
QK_ROPE = 32
QK_HEAD = QK_NOPE + QK_ROPE
V_HEAD = 64
Q_LORA = 256
KV_LORA = 128
ROPE_THETA = 10000.0
CONV_WIDTH = 512
SG_WIDTH = 512
SG_GROUPS = 4
SG_CHUNK = 128
MEM_HEADS = 4
MEM_HEAD_DIM = 128
N_BRANCH = 4
BRANCH_WIDTH = 512
NEG_INF = -1e30
LOG2E = 1.4426950408889634

LANES = 128
HALF_ROPE = QK_ROPE // 2
OFF_CONV = Q_LORA + KV_LORA + QK_ROPE
C_CONV = 0
C_SG = C_CONV + 3 * CONV_WIDTH
C_MEMQ = C_SG + 2 * SG_WIDTH
C_SILU = C_MEMQ + MEM_HEADS * MEM_HEAD_DIM
C_LOGIT = C_SILU + N_BRANCH * BRANCH_WIDTH
C_END = C_LOGIT + N_BRANCH * D_MODEL

TILE_ROPE = 512
TILE_LAYER = 256
VMEM_LIMIT_BYTES = 56 * 1024 * 1024

BF16 = jnp.bfloat16
F32 = jnp.float32


def _dot(a, b):
    return jnp.dot(a, b, preferred_element_type=F32)


def _dot_nt(a, b):
    return lax.dot_general(a, b, (((1,), (1,)), ((), ())), preferred_element_type=F32)


def _rms(x, g):
    return x * lax.rsqrt(jnp.mean(x * x, axis=-1, keepdims=True) + EPS) * g


def _sigmoid(x):
    return 1.0 / (1.0 + jnp.exp(-x))


def _silu(x):
    return x * _sigmoid(x)


def _rope_table_kernel(pos_ref, invf_ref, cos_ref, sin_ref, cost_ref, sint_ref):
    pos = pos_ref[0].astype(F32)
    ang = pos * invf_ref[...]
    lane = lax.broadcasted_iota(jnp.int32, ang.shape, 1)
    c = jnp.cos(ang)
    s = jnp.sin(ang)
    c = jnp.where(lane < QK_NOPE, 1.0, jnp.where(lane < QK_HEAD, c, 0.0))
    s = jnp.where(lane < QK_NOPE, 0.0,
                  jnp.where(lane < QK_NOPE + HALF_ROPE, -s, jnp.where(lane < QK_HEAD, s, 0.0)))
    cos_ref[0] = c
    sin_ref[0] = s
    cost_ref[0] = c.T
    sint_ref[0] = s.T


def _rope_tables(positions):
    b, s = positions.shape
    half = jnp.arange(HALF_ROPE, dtype=F32)
    inv_freq = ROPE_THETA ** (-half / HALF_ROPE)
    invf = jnp.concatenate([jnp.zeros((QK_NOPE,), F32), inv_freq, inv_freq,
                            jnp.zeros((LANES - QK_HEAD,), F32)])[None, :]
    t = TILE_ROPE
    return pl.pallas_call(
        _rope_table_kernel,
        grid=(b, s // t),
        in_specs=[pl.BlockSpec((1, t, 1), lambda i, j: (i, j, 0)),
                  pl.BlockSpec((1, LANES), lambda i, j: (0, 0))],
        out_specs=[pl.BlockSpec((1, t, LANES), lambda i, j: (i, j, 0))] * 2
        + [pl.BlockSpec((1, LANES, t), lambda i, j: (i, 0, j))] * 2,
        out_shape=[jax.ShapeDtypeStruct((b, s, LANES), F32)] * 2
        + [jax.ShapeDtypeStruct((b, LANES, s), F32)] * 2,
        name="rope_tables",
    )(positions[:, :, None], invf)


def _mem_kv_kernel(mem_ref, mg_ref, w_ref, kg_ref, k_ref, v_ref):
    mn = _rms(mem_ref[0], mg_ref[...]).astype(BF16)
    kv = _dot(mn, w_ref[...])
    for hd in range(MEM_HEADS):
        sl = slice(hd * MEM_HEAD_DIM, (hd + 1) * MEM_HEAD_DIM)
        k_ref[0, hd] = _rms(kv[:, sl], kg_ref[...]).astype(BF16)
        off = MEM_HEADS * MEM_HEAD_DIM
        v_ref[0, hd] = kv[:, off + hd * MEM_HEAD_DIM:off + (hd + 1) * MEM_HEAD_DIM].astype(BF16)


def _mem_kv(mem, mg, w, kg):
    b, m, d = mem.shape
    const = lambda shape: pl.BlockSpec(shape, lambda i: (0,) * len(shape))
    return pl.pallas_call(
        _mem_kv_kernel,
        grid=(b,),
        in_specs=[pl.BlockSpec((1, m, d), lambda i: (i, 0, 0)), const(mg.shape), const(w.shape), const(kg.shape)],
        out_specs=[pl.BlockSpec((1, MEM_HEADS, m, MEM_HEAD_DIM), lambda i: (i, 0, 0, 0))] * 2,
        out_shape=[jax.ShapeDtypeStruct((b, MEM_HEADS, m, MEM_HEAD_DIM), BF16)] * 2,
        compiler_params=pltpu.CompilerParams(dimension_semantics=("arbitrary",),
                                             vmem_limit_bytes=VMEM_LIMIT_BYTES),
        name="mem_kv",
    )(mem, mg, w, kg)


def _layer_kernel(x_ref, ng_ref, wc_ref, wa_ref, cqg_ref, ckvg_ref, wuqt_ref, wuk_ref, wuvt_ref, gq_ref, gk_ref,
                  cos_ref, sin_ref, cost_ref, sint_ref, km_ref, vm_ref,
                  convw_ref, convb_ref, lng_ref, lnb_ref, ws_ref, bs_ref, mqg_ref,
                  bm_ref, wb_ref, wo_ref, out_ref,
                  h_ref, ysg_ref, ztail_ref, q_scr, k_scr, vt_scr, m_ref, l_ref, acc_ref, mix_ref, macc_ref,
                  merged_ref):
    tm = TILE_LAYER
    i = pl.program_id(1)
    rows_i = pl.ds(pl.multiple_of(i * tm, tm), tm)
    h_ref[...] = _rms(x_ref[0], ng_ref[...]).astype(BF16)

    def proj(col, width):
        return _dot(h_ref[...], wc_ref[:, col:col + width])

    pa = _dot(h_ref[...], wa_ref[...])
    cq = _rms(pa[:, :Q_LORA], cqg_ref[...]).astype(BF16)
    ckv = _rms(pa[:, Q_LORA:Q_LORA + KV_LORA], ckvg_ref[...]).astype(BF16)
    kr = pa[:, Q_LORA + KV_LORA:Q_LORA + KV_LORA + LANES]
    krp = pa[:, Q_LORA + KV_LORA + LANES:]
    qt_all = _dot_nt(wuqt_ref[...], cq)
    k_all = _dot(ckv, wuk_ref[...])
    vt = _dot_nt(wuvt_ref[...], ckv).astype(BF16)

    qscale = QK_HEAD ** -0.5 * LOG2E
    rope_rows = slice(QK_NOPE, QK_HEAD)
    gq_nope = gq_ref[0:QK_NOPE, 0:1] * qscale
    cq_t = (gq_ref[rope_rows, 0:1] * qscale) * cost_ref[0, rope_rows, :]
    sq_t = (gq_ref[rope_rows, 1:2] * qscale) * sint_ref[0, rope_rows, :]
    for hd in range(MLA_HEADS):
        nope = qt_all[hd * LANES:hd * LANES + QK_NOPE, :]
        rope = qt_all[hd * LANES + QK_NOPE:hd * LANES + QK_HEAD, :]
        ss = jnp.sum(nope * nope, axis=0, keepdims=True) + jnp.sum(rope * rope, axis=0, keepdims=True)
        rinv = lax.rsqrt(ss / QK_HEAD + EPS)
        partner = jnp.concatenate([rope[HALF_ROPE:, :], rope[:HALF_ROPE, :]], axis=0)
        q_scr[hd, 0:QK_NOPE, :] = (nope * gq_nope * rinv).astype(BF16)
        q_scr[hd, rope_rows, :] = ((rope * cq_t + partner * sq_t) * rinv).astype(BF16)
        q_scr[hd, QK_HEAD:, :] = jnp.zeros((LANES - QK_HEAD, tm), BF16)

    gck = gk_ref[0:1, :] * cos_ref[0]
    krp_s = krp * (gk_ref[1:2, :] * sin_ref[0])
    for hd in range(MLA_HEADS):
        ksl = k_all[:, hd * LANES:(hd + 1) * LANES] + kr
        ss = jnp.sum(ksl * ksl, axis=-1, keepdims=True)
        k_scr[hd, rows_i, :] = ((ksl * gck + krp_s) * lax.rsqrt(ss / QK_HEAD + EPS)).astype(BF16)
        vt_scr[hd, i] = vt[hd * V_HEAD:(hd + 1) * V_HEAD, :]

    @pl.when(i == 0)
    def _():
        ztail_ref[...] = jnp.zeros_like(ztail_ref)

    half = CONV_WIDTH // 2
    for c in range(2):
        cs = slice(c * half, (c + 1) * half)
        bg = proj(C_CONV + c * half, half)
        cg = proj(C_CONV + CONV_WIDTH + c * half, half)
        xin = proj(C_CONV + 2 * CONV_WIDTH + c * half, half)
        z = cg * xin
        tail = ztail_ref[:, cs]
        row = lax.broadcasted_iota(jnp.int32, z.shape, 0)
        z1 = jnp.where(row == 0, tail[7:8, :], pltpu.roll(z, 1, axis=0))
        z2 = jnp.where(row == 0, tail[6:7, :], jnp.where(row == 1, tail[7:8, :], pltpu.roll(z, 2, axis=0)))
        y = convb_ref[:, cs] + convw_ref[0:1, cs] * z2
        y = y + convw_ref[1:2, cs] * z1
        y = y + convw_ref[2:3, cs] * z
        ztail_ref[:, cs] = z[tm - 8:tm, :]
        gate = proj(C_SILU + BRANCH_WIDTH + c * half, half)
        ysg_ref[1, :, cs] = (bg * y * _silu(gate)).astype(BF16)

    v = proj(C_SG + SG_WIDTH, SG_WIDTH)
    mu = jnp.mean(v, axis=-1, keepdims=True)
    vc = v - mu
    vn = (vc * lax.rsqrt(jnp.mean(vc * vc, axis=-1, keepdims=True) + EPS) * lng_ref[...] + lnb_ref[...]).astype(BF16)
    low = (lax.broadcasted_iota(jnp.int32, (SG_CHUNK, SG_CHUNK), 1)
           <= lax.broadcasted_iota(jnp.int32, (SG_CHUNK, SG_CHUNK), 0))
    gw = SG_WIDTH // SG_GROUPS
    for g in range(SG_GROUPS):
        wt = jnp.where(low, ws_ref[g], 0.0).astype(BF16)
        for c in range(tm // SG_CHUNK):
            rs = slice(c * SG_CHUNK, (c + 1) * SG_CHUNK)
            gs = slice(g * gw, (g + 1) * gw)
            mix_ref[rs, gs] = _dot(wt, vn[rs, gs]) + bs_ref[:, gs]
    u = proj(C_SG, SG_WIDTH)
    ysg_ref[2] = (u * mix_ref[...] * _silu(proj(C_SILU + 2 * BRANCH_WIDTH, BRANCH_WIDTH))).astype(BF16)

    mq = proj(C_MEMQ, MEM_HEADS * MEM_HEAD_DIM)
    outs = []
    for hd in range(MEM_HEADS):
        qh = (_rms(mq[:, hd * MEM_HEAD_DIM:(hd + 1) * MEM_HEAD_DIM], mqg_ref[...])
              * (MEM_HEAD_DIM ** -0.5)).astype(BF16)
        s = _dot_nt(qh, km_ref[0, hd])
        p = jnp.exp(s - jnp.max(s, axis=-1, keepdims=True))
        l = jnp.sum(p, axis=-1, keepdims=True)
        outs.append(_dot(p.astype(BF16), vm_ref[0, hd]) * (1.0 / l))
    yd = jnp.concatenate(outs, axis=1)
    ysg_ref[3] = (yd * _silu(proj(C_SILU + 3 * BRANCH_WIDTH, BRANCH_WIDTH))).astype(BF16)

    m_ref[...] = jnp.full(m_ref.shape, NEG_INF, F32)
    l_ref[...] = jnp.zeros(l_ref.shape, F32)
    acc_ref[...] = jnp.zeros(acc_ref.shape, F32)

    def block_scores(kb):
        rows = pl.ds(pl.multiple_of(kb * tm, tm), tm)
        return [_dot(k_scr[hd, rows, :], q_scr[hd]) for hd in range(MLA_HEADS)]

    def block_update(kb, scores, mask):
        new = []
        for hd in range(MLA_HEADS):
            st = scores[hd]
            if mask:
                key = lax.broadcasted_iota(jnp.int32, (tm, tm), 0)
                qry = lax.broadcasted_iota(jnp.int32, (tm, tm), 1)
                st = jnp.where(key <= qry, st, NEG_INF)
            m_old = m_ref[hd]
            m_new = jnp.maximum(m_old, jnp.max(st, axis=0, keepdims=True))
            alpha = jnp.exp2(m_old - m_new)
            p = jnp.exp2(st - m_new)
            new.append((m_new, alpha * l_ref[hd] + jnp.sum(p, axis=0, keepdims=True),
                        alpha * acc_ref[hd] + _dot(vt_scr[hd, kb], p.astype(BF16))))
        for hd, (m_new, l_new, acc_new) in enumerate(new):
            m_ref[hd] = m_new
            l_ref[hd] = l_new
            acc_ref[hd] = acc_new

    def kb_body(kb, carry):
        block_update(kb, block_scores(kb), False)
        return carry

    lax.fori_loop(0, i, kb_body, 0)

    diag = block_scores(i)
    nc = D_MODEL // 2

    def gated(n, c):
        zc = _dot(ysg_ref[n], wb_ref[n, :, c * nc:(c + 1) * nc])
        gate = _sigmoid(proj(C_LOGIT + n * D_MODEL + c * nc, nc) + bm_ref[n:n + 1, c * nc:(c + 1) * nc])
        return gate * zc

    for c in range(2):
        macc_ref[:, c * nc:(c + 1) * nc] = gated(1, c) + gated(2, c) + gated(3, c)
    silu0 = _silu(proj(C_SILU, BRANCH_WIDTH))
    block_update(i, diag, True)
    yat = jnp.concatenate([acc_ref[hd] * (1.0 / l_ref[hd]) for hd in range(MLA_HEADS)], axis=0)
    ysg_ref[0] = (yat.T * silu0).astype(BF16)

    for c in range(2):
        merged_ref[:, c * nc:(c + 1) * nc] = (macc_ref[:, c * nc:(c + 1) * nc] + gated(0, c)).astype(BF16)
    out_ref[0] = x_ref[0] + _dot(merged_ref[...], wo_ref[...])


def _layer(l, x, ng, wc, wa, cqg, ckvg, wuqt, wuk, wuvt, gq, gk, cos, sin, cost, sint, km, vm, convw, convb, lng, lnb,
           ws, bs, mqg, bm, wb, wo):
    b, s, d = x.shape
    tm = TILE_LAYER

    def const(shape):
        return pl.BlockSpec(shape, lambda i, j: (0,) * len(shape), pipeline_mode=pl.Buffered(1))

    def layer_const(shape):
        return pl.BlockSpec((None,) + shape[1:], lambda i, j: (l,) + (0,) * (len(shape) - 1),
                            pipeline_mode=pl.Buffered(1))

    def per_batch(shape):
        return pl.BlockSpec((1,) + shape[1:], lambda i, j: (i,) + (0,) * (len(shape) - 1),
                            pipeline_mode=pl.Buffered(1))

    tile = lambda width: pl.BlockSpec((1, tm, width), lambda i, j: (i, j, 0))
    tile_t = pl.BlockSpec((1, LANES, tm), lambda i, j: (i, 0, j))
    return pl.pallas_call(
        _layer_kernel,
        grid=(b, s // tm),
        in_specs=[tile(d), const(ng.shape), layer_const(wc.shape),
                  const(wa.shape), const(cqg.shape), const(ckvg.shape), const(wuqt.shape), const(wuk.shape),
                  const(wuvt.shape), const(gq.shape), const(gk.shape), tile(LANES), tile(LANES), tile_t, tile_t,
                  per_batch(km.shape), per_batch(vm.shape),
                  const(convw.shape), const(convb.shape), const(lng.shape), const(lnb.shape),
                  const(ws.shape), const(bs.shape), const(mqg.shape), const(bm.shape),
                  layer_const(wb.shape), layer_const(wo.shape)],
        out_specs=tile(d),
        out_shape=jax.ShapeDtypeStruct((b, s, d), F32),
        scratch_shapes=[pltpu.VMEM((tm, d), BF16),
                        pltpu.VMEM((N_BRANCH, tm, BRANCH_WIDTH), BF16),
                        pltpu.VMEM((8, CONV_WIDTH), F32),
                        pltpu.VMEM((MLA_HEADS, LANES, tm), BF16),
                        pltpu.VMEM((MLA_HEADS, s, LANES), BF16),
                        pltpu.VMEM((MLA_HEADS, s // tm, V_HEAD, tm), BF16),
                        pltpu.VMEM((MLA_HEADS, 1, tm), F32),
                        pltpu.VMEM((MLA_HEADS, 1, tm), F32),
                        pltpu.VMEM((MLA_HEADS, V_HEAD, tm), F32),
                        pltpu.VMEM((tm, SG_WIDTH), F32),
                        pltpu.VMEM((tm, d), F32),
                        pltpu.VMEM((tm, d), BF16)],
        compiler_params=pltpu.CompilerParams(dimension_semantics=("arbitrary", "arbitrary"),
                                             vmem_limit_bytes=VMEM_LIMIT_BYTES),
        name="hybrid_layer",
    )(x, ng, wc, wa, cqg, ckvg, wuqt, wuk, wuvt, gq, gk, cos, sin, cost, sint, km, vm, convw, convb, lng, lnb, ws, bs,
      mqg, bm, wb, wo)


def _swap_rope_halves(r):
    return jnp.concatenate([r[..., HALF_ROPE:], r[..., :HALF_ROPE]], axis=-1)


def _slab(nope, rope):
    pad = jnp.zeros(rope.shape[:-1] + (LANES - QK_HEAD,), rope.dtype)
    return jnp.concatenate([nope, rope, pad], axis=-1)


def _head_gain_slab(g):
    rope = g[QK_NOPE:]
    return jnp.stack([_slab(g[:QK_NOPE], rope), _slab(jnp.zeros((QK_NOPE,), g.dtype), _swap_rope_halves(rope))])


def _prep_layer(w_in, w_uq, w_ukv, q_g, k_g, b_spatial):
    kr = w_in[:, Q_LORA + KV_LORA:OFF_CONV]
    no_nope = jnp.zeros((D_MODEL, QK_NOPE), F32)
    wa = jnp.concatenate([w_in[:, :Q_LORA + KV_LORA], _slab(no_nope, kr), _slab(no_nope, _swap_rope_halves(kr))],
                         axis=1).astype(BF16)
    uq = w_uq.reshape(Q_LORA, MLA_HEADS, QK_HEAD)
    wuqt = _slab(uq[..., :QK_NOPE], uq[..., QK_NOPE:]).reshape(Q_LORA, MLA_HEADS * LANES).T.astype(BF16)
    ukv = w_ukv.reshape(KV_LORA, MLA_HEADS, QK_NOPE + V_HEAD)
    wuk = jnp.concatenate([ukv[..., :QK_NOPE], jnp.zeros((KV_LORA, MLA_HEADS, LANES - QK_NOPE), F32)], axis=-1)
    wuk = wuk.reshape(KV_LORA, MLA_HEADS * LANES).astype(BF16)
    wuvt = ukv[..., QK_NOPE:].reshape(KV_LORA, MLA_HEADS * V_HEAD).T.astype(BF16)
    bs = jnp.repeat(b_spatial.T, SG_WIDTH // SG_GROUPS, axis=1)
    return wa, wuqt, wuk, wuvt, _head_gain_slab(q_g).T, _head_gain_slab(k_g), bs


def kernel(x, mem, positions, norm_g, w_in, cq_norm_g, ckv_norm_g, w_uq, w_ukv, mla_q_norm_g, mla_k_norm_g,
           conv_w, conv_b, sg_ln_g, sg_ln_b, w_spatial, b_spatial, mem_norm_g, w_mem_kv, mem_q_norm_g,
           mem_k_norm_g, b_merge, w_branch, w_out):
    assert x.shape[1] % TILE_ROPE == 0 and x.shape[2] == D_MODEL
    cos, sin, cost, sint = _rope_tables(positions)
    wc_all = w_in[:, :, OFF_CONV:].astype(BF16)
    wb_all = w_branch.astype(BF16)
    wo_all = w_out.astype(BF16)
    row = lambda a: a[None, :]
    for l in range(DEPTH):
        wa, wuqt, wuk, wuvt, gq, gk, bs = _prep_layer(w_in[l], w_uq[l], w_ukv[l], mla_q_norm_g[l],
                                                      mla_k_norm_g[l], b_spatial[l])
        km, vm = _mem_kv(mem, row(mem_norm_g[l]), w_mem_kv[l].astype(BF16), row(mem_k_norm_g[l]))
        x = _layer(l, x, row(norm_g[l]), wc_all, wa, row(cq_norm_g[l]), row(ckv_norm_g[l]), wuqt, wuk, wuvt, gq, gk,
                   cos, sin, cost, sint, km, vm, conv_w[l], row(conv_b[l]), row(sg_ln_g[l]), row(sg_ln_b[l]),
                   w_spatial[l], bs, row(mem_q_norm_g[l]), b_merge[l], wb_all, wo_all)
    return x
```

```python
import functools

import jax
import jax.numpy as jnp
from jax import lax
from jax.experimental import pallas as pl
from jax.experimental.pallas import tpu as pltpu

D_MODEL = 1024
DEPTH = 2
EPS = 1e-6
MLA_HEADS = 8
QK_NOPE = 64
QK_ROPE = 32
QK_HEAD = QK_NOPE + QK_ROPE
V_HEAD = 64
Q_LORA = 256
KV_LORA = 128
ROPE_THETA = 10000.0
CONV_WIDTH = 512
SG_WIDTH = 512
SG_GROUPS = 4
SG_CHUNK = 128
MEM_HEADS = 4
MEM_HEAD_DIM = 128
N_BRANCH = 4
BRANCH_WIDTH = 512
NEG_INF = -1e30
LOG2E = 1.4426950408889634

LANES = 128
HALF_ROPE = QK_ROPE // 2
OFF_CONV = Q_LORA + KV_LORA + QK_ROPE
C_CONV = 0
C_SG = C_CONV + 3 * CONV_WIDTH
C_MEMQ = C_SG + 2 * SG_WIDTH
C_SILU = C_MEMQ + MEM_HEADS * MEM_HEAD_DIM
C_LOGIT = C_SILU + N_BRANCH * BRANCH_WIDTH
C_END = C_LOGIT + N_BRANCH * D_MODEL

TILE_ROPE = 512
TILE_LAYER = 256
VMEM_LIMIT_BYTES = 56 * 1024 * 1024

BF16 = jnp.bfloat16
F32 = jnp.float32


def _dot(a, b):
    return jnp.dot(a, b, preferred_element_type=F32)


def _dot_nt(a, b):
    return lax.dot_general(a, b, (((1,), (1,)), ((), ())), preferred_element_type=F32)


def _rms(x, g):
    return x * lax.rsqrt(jnp.mean(x * x, axis=-1, keepdims=True) + EPS) * g


def _sigmoid(x):
    return 1.0 / (1.0 + jnp.exp(-x))


def _silu(x):
    return x * _sigmoid(x)


def _rope_table_kernel(pos_ref, invf_ref, cos_ref, sin_ref, cost_ref, sint_ref):
    pos = pos_ref[0].astype(F32)
    ang = invf_ref[...] * pos
    t = ang.shape[1]
    row = lax.broadcasted_iota(jnp.int32, ang.shape, 0)
    c = jnp.cos(ang)
    s = jnp.sin(ang)
    s = jnp.where(row < HALF_ROPE, -s, s)
    cost_ref[0] = c
    sint_ref[0] = s
    pad = jnp.zeros((LANES - QK_HEAD, t), F32)
    cos_ref[0] = jnp.concatenate([jnp.ones((QK_NOPE, t), F32), c, pad], axis=0).T
    sin_ref[0] = jnp.concatenate([jnp.zeros((QK_NOPE, t), F32), s, pad], axis=0).T


def _rope_tables(positions):
    b, s = positions.shape
    half = jnp.arange(HALF_ROPE, dtype=F32)
    inv_freq = ROPE_THETA ** (-half / HALF_ROPE)
    invf = jnp.concatenate([inv_freq, inv_freq])[:, None]
    t = TILE_ROPE
    return pl.pallas_call(
        _rope_table_kernel,
        grid=(b, s // t),
        in_specs=[pl.BlockSpec((1, 1, t), lambda i, j: (i, 0, j)),
                  pl.BlockSpec((QK_ROPE, 1), lambda i, j: (0, 0))],
        out_specs=[pl.BlockSpec((1, t, LANES), lambda i, j: (i, j, 0))] * 2
        + [pl.BlockSpec((1, QK_ROPE, t), lambda i, j: (i, 0, j))] * 2,
        out_shape=[jax.ShapeDtypeStruct((b, s, LANES), F32)] * 2
        + [jax.ShapeDtypeStruct((b, QK_ROPE, s), F32)] * 2,
        name="rope_tables",
    )(positions[:, None, :], invf)


def _mem_kv_kernel(mem_ref, mg_ref, w_ref, kg_ref, k_ref, v_ref):
    mn = _rms(mem_ref[0], mg_ref[...]).astype(BF16)
    kv = _dot(mn, w_ref[...])
    for hd in range(MEM_HEADS):
        sl = slice(hd * MEM_HEAD_DIM, (hd + 1) * MEM_HEAD_DIM)
        k_ref[0, hd] = _rms(kv[:, sl], kg_ref[...]).astype(BF16)
        off = MEM_HEADS * MEM_HEAD_DIM
        v_ref[0, hd] = kv[:, off + hd * MEM_HEAD_DIM:off + (hd + 1) * MEM_HEAD_DIM].astype(BF16)


def _mem_kv(mem, mg, w, kg):
    b, m, d = mem.shape
    const = lambda shape: pl.BlockSpec(shape, lambda i: (0,) * len(shape))
    return pl.pallas_call(
        _mem_kv_kernel,
        grid=(b,),
        in_specs=[pl.BlockSpec((1, m, d), lambda i: (i, 0, 0)), const(mg.shape), const(w.shape), const(kg.shape)],
        out_specs=[pl.BlockSpec((1, MEM_HEADS, m, MEM_HEAD_DIM), lambda i: (i, 0, 0, 0))] * 2,
        out_shape=[jax.ShapeDtypeStruct((b, MEM_HEADS, m, MEM_HEAD_DIM), BF16)] * 2,
        compiler_params=pltpu.CompilerParams(dimension_semantics=("arbitrary",),
                                             vmem_limit_bytes=VMEM_LIMIT_BYTES),
        name="mem_kv",
    )(mem, mg, w, kg)


def _layer_kernel(x_ref, ng_ref, wc_ref, wa_ref, cqg_ref, ckvg_ref, wuqt_ref, wuk_ref, wuvt_ref, gq_ref, gk_ref,
                  cos_ref, sin_ref, cost_ref, sint_ref, km_ref, vm_ref,
                  convw_ref, convb_ref, lng_ref, lnb_ref, ws_ref, bs_ref, mqg_ref,
                  bm_ref, wb_ref, wo_ref, out_ref,
                  h_ref, ysg_ref, ztail_ref, q_scr, k_scr, vt_scr, m_ref, l_ref, acc_ref, mix_ref, macc_ref,
                  merged_ref):
    tm = TILE_LAYER
    i = pl.program_id(1)
    rows_i = pl.ds(pl.multiple_of(i * tm, tm), tm)
    h_ref[...] = _rms(x_ref[0], ng_ref[...]).astype(BF16)

    def proj(col, width):
        return _dot(h_ref[...], wc_ref[:, col:col + width])

    pa = _dot(h_ref[...], wa_ref[...])
    cq = _rms(pa[:, :Q_LORA], cqg_ref[...]).astype(BF16)
    ckv = _rms(pa[:, Q_LORA:Q_LORA + KV_LORA], ckvg_ref[...]).astype(BF16)
    kr = pa[:, Q_LORA + KV_LORA:Q_LORA + KV_LORA + LANES]
    krp = pa[:, Q_LORA + KV_LORA + LANES:]
    qt_all = _dot_nt(wuqt_ref[...], cq)
    k_all = _dot(ckv, wuk_ref[...])
    vt = _dot_nt(wuvt_ref[...], ckv).astype(BF16)

    qscale = QK_HEAD ** -0.5 * LOG2E
    rope_rows = slice(QK_NOPE, QK_HEAD)
    gq_nope = gq_ref[0:QK_NOPE, 0:1] * qscale
    cq_t = (gq_ref[rope_rows, 0:1] * qscale) * cost_ref[0]
    sq_t = (gq_ref[rope_rows, 1:2] * qscale) * sint_ref[0]
    for hd in range(MLA_HEADS):
        nope = qt_all[hd * LANES:hd * LANES + QK_NOPE, :]
        rope = qt_all[hd * LANES + QK_NOPE:hd * LANES + QK_HEAD, :]
        ss = jnp.sum(nope * nope, axis=0, keepdims=True) + jnp.sum(rope * rope, axis=0, keepdims=True)
        rinv = lax.rsqrt(ss / QK_HEAD + EPS)
        partner = jnp.concatenate([rope[HALF_ROPE:, :], rope[:HALF_ROPE, :]], axis=0)
        q_scr[hd, 0:QK_NOPE, :] = (nope * gq_nope * rinv).astype(BF16)
        q_scr[hd, rope_rows, :] = ((rope * cq_t + partner * sq_t) * rinv).astype(BF16)
        q_scr[hd, QK_HEAD:, :] = jnp.zeros((LANES - QK_HEAD, tm), BF16)

    gck = gk_ref[0:1, :] * cos_ref[0]
    krp_s = krp * (gk_ref[1:2, :] * sin_ref[0])
    for hd in range(MLA_HEADS):
        ksl = k_all[:, hd * LANES:(hd + 1) * LANES] + kr
        ss = jnp.sum(ksl * ksl, axis=-1, keepdims=True)
        k_scr[hd, rows_i, :] = ((ksl * gck + krp_s) * lax.rsqrt(ss / QK_HEAD + EPS)).astype(BF16)
        vt_scr[hd, i] = vt[hd * V_HEAD:(hd + 1) * V_HEAD, :]

    @pl.when(i == 0)
    def _():
        ztail_ref[...] = jnp.zeros_like(ztail_ref)

    half = CONV_WIDTH // 2
    for c in range(2):
        cs = slice(c * half, (c + 1) * half)
        bg = proj(C_CONV + c * half, half)
        cg = proj(C_CONV + CONV_WIDTH + c * half, half)
        xin = proj(C_CONV + 2 * CONV_WIDTH + c * half, half)
        z = cg * xin
        tail = ztail_ref[:, cs]
        row = lax.broadcasted_iota(jnp.int32, z.shape, 0)
        z1 = jnp.where(row == 0, tail[7:8, :], pltpu.roll(z, 1, axis=0))
        z2 = jnp.where(row == 0, tail[6:7, :], jnp.where(row == 1, tail[7:8, :], pltpu.roll(z, 2, axis=0)))
        y = convb_ref[:, cs] + convw_ref[0:1, cs] * z2
        y = y + convw_ref[1:2, cs] * z1
        y = y + convw_ref[2:3, cs] * z
        ztail_ref[:, cs] = z[tm - 8:tm, :]
        gate = proj(C_SILU + BRANCH_WIDTH + c * half, half)
        ysg_ref[1, :, cs] = (bg * y * _silu(gate)).astype(BF16)

    v = proj(C_SG + SG_WIDTH, SG_WIDTH)
    mu = jnp.mean(v, axis=-1, keepdims=True)
    vc = v - mu
    vn = (vc * lax.rsqrt(jnp.mean(vc * vc, axis=-1, keepdims=True) + EPS) * lng_ref[...] + lnb_ref[...]).astype(BF16)
    low = (lax.broadcasted_iota(jnp.int32, (SG_CHUNK, SG_CHUNK), 1)
           <= lax.broadcasted_iota(jnp.int32, (SG_CHUNK, SG_CHUNK), 0))
    gw = SG_WIDTH // SG_GROUPS
    for g in range(SG_GROUPS):
        wt = jnp.where(low, ws_ref[g], 0.0).astype(BF16)
        for c in range(tm // SG_CHUNK):
            rs = slice(c * SG_CHUNK, (c + 1) * SG_CHUNK)
            gs = slice(g * gw, (g + 1) * gw)
            mix_ref[rs, gs] = _dot(wt, vn[rs, gs]) + bs_ref[:, gs]
    u = proj(C_SG, SG_WIDTH)
    ysg_ref[2] = (u * mix_ref[...] * _silu(proj(C_SILU + 2 * BRANCH_WIDTH, BRANCH_WIDTH))).astype(BF16)

    mq = proj(C_MEMQ, MEM_HEADS * MEM_HEAD_DIM)
    outs = []
    for hd in range(MEM_HEADS):
        qh = (_rms(mq[:, hd * MEM_HEAD_DIM:(hd + 1) * MEM_HEAD_DIM], mqg_ref[...])
              * (MEM_HEAD_DIM ** -0.5)).astype(BF16)
        s = _dot_nt(qh, km_ref[0, hd])
        p = jnp.exp(s - jnp.max(s, axis=-1, keepdims=True))
        l = jnp.sum(p, axis=-1, keepdims=True)
        outs.append(_dot(p.astype(BF16), vm_ref[0, hd]) * (1.0 / l))
    yd = jnp.concatenate(outs, axis=1)
    ysg_ref[3] = (yd * _silu(proj(C_SILU + 3 * BRANCH_WIDTH, BRANCH_WIDTH))).astype(BF16)

    m_ref[...] = jnp.full(m_ref.shape, NEG_INF, F32)
    l_ref[...] = jnp.zeros(l_ref.shape, F32)
    acc_ref[...] = jnp.zeros(acc_ref.shape, F32)

    def block_scores(kb):
        rows = pl.ds(pl.multiple_of(kb * tm, tm), tm)
        return [_dot(k_scr[hd, rows, :], q_scr[hd]) for hd in range(MLA_HEADS)]

    def block_update(kb, scores, mask):
        new = []
        for hd in range(MLA_HEADS):
            st = scores[hd]
            if mask:
                key = lax.broadcasted_iota(jnp.int32, (tm, tm), 0)
                qry = lax.broadcasted_iota(jnp.int32, (tm, tm), 1)
                st = jnp.where(key <= qry, st, NEG_INF)
            m_old = m_ref[hd]
            m_new = jnp.maximum(m_old, jnp.max(st, axis=0, keepdims=True))
            alpha = jnp.exp2(m_old - m_new)
            p = jnp.exp2(st - m_new)
            new.append((m_new, alpha * l_ref[hd] + jnp.sum(p, axis=0, keepdims=True),
                        alpha * acc_ref[hd] + _dot(vt_scr[hd, kb], p.astype(BF16))))
        for hd, (m_new, l_new, acc_new) in enumerate(new):
            m_ref[hd] = m_new
            l_ref[hd] = l_new
            acc_ref[hd] = acc_new

    def kb_body(kb, carry):
        block_update(kb, block_scores(kb), False)
        return carry

    lax.fori_loop(0, i, kb_body, 0)

    diag = block_scores(i)
    nc = D_MODEL // 2

    def gated(n, c):
        zc = _dot(ysg_ref[n], wb_ref[n, :, c * nc:(c + 1) * nc])
        gate = _sigmoid(proj(C_LOGIT + n * D_MODEL + c * nc, nc) + bm_ref[n:n + 1, c * nc:(c + 1) * nc])
        return gate * zc

    for c in range(2):
        macc_ref[:, c * nc:(c + 1) * nc] = gated(1, c) + gated(2, c) + gated(3, c)
    silu0 = _silu(proj(C_SILU, BRANCH_WIDTH))
    block_update(i, diag, True)
    yat = jnp.concatenate([acc_ref[hd] * (1.0 / l_ref[hd]) for hd in range(MLA_HEADS)], axis=0)
    ysg_ref[0] = (yat.T * silu0).astype(BF16)

    for c in range(2):
        merged_ref[:, c * nc:(c + 1) * nc] = (macc_ref[:, c * nc:(c + 1) * nc] + gated(0, c)).astype(BF16)
    out_ref[0] = x_ref[0] + _dot(merged_ref[...], wo_ref[...])


def _layer(x, ng, wc, wa, cqg, ckvg, wuqt, wuk, wuvt, gq, gk, cos, sin, cost, sint, km, vm, convw, convb, lng, lnb,
           ws, bs, mqg, bm, wb, wo):
    b, s, d = x.shape
    tm = TILE_LAYER

    def const(shape):
        return pl.BlockSpec(shape, lambda i, j: (0,) * len(shape), pipeline_mode=pl.Buffered(1))

    def per_batch(shape):
        return pl.BlockSpec((1,) + shape[1:], lambda i, j: (i,) + (0,) * (len(shape) - 1),
                            pipeline_mode=pl.Buffered(1))

    tile = lambda width: pl.BlockSpec((1, tm, width), lambda i, j: (i, j, 0))
    tile_t = pl.BlockSpec((1, QK_ROPE, tm), lambda i, j: (i, 0, j))
    return pl.pallas_call(
        _layer_kernel,
        grid=(b, s // tm),
        in_specs=[tile(d), const(ng.shape), const(wc.shape),
                  const(wa.shape), const(cqg.shape), const(ckvg.shape), const(wuqt.shape), const(wuk.shape),
                  const(wuvt.shape), const(gq.shape), const(gk.shape), tile(LANES), tile(LANES), tile_t, tile_t,
                  per_batch(km.shape), per_batch(vm.shape),
                  const(convw.shape), const(convb.shape), const(lng.shape), const(lnb.shape),
                  const(ws.shape), const(bs.shape), const(mqg.shape), const(bm.shape),
                  const(wb.shape), const(wo.shape)],
        out_specs=tile(d),
        out_shape=jax.ShapeDtypeStruct((b, s, d), F32),
        scratch_shapes=[pltpu.VMEM((tm, d), BF16),
                        pltpu.VMEM((N_BRANCH, tm, BRANCH_WIDTH), BF16),
                        pltpu.VMEM((8, CONV_WIDTH), F32),
                        pltpu.VMEM((MLA_HEADS, LANES, tm), BF16),
                        pltpu.VMEM((MLA_HEADS, s, LANES), BF16),
                        pltpu.VMEM((MLA_HEADS, s // tm, V_HEAD, tm), BF16),
                        pltpu.VMEM((MLA_HEADS, 1, tm), F32),
                        pltpu.VMEM((MLA_HEADS, 1, tm), F32),
                        pltpu.VMEM((MLA_HEADS, V_HEAD, tm), F32),
                        pltpu.VMEM((tm, SG_WIDTH), F32),
                        pltpu.VMEM((tm, d), F32),
                        pltpu.VMEM((tm, d), BF16)],
        compiler_params=pltpu.CompilerParams(dimension_semantics=("arbitrary", "arbitrary"),
                                             vmem_limit_bytes=VMEM_LIMIT_BYTES),
        name="hybrid_layer",
    )(x, ng, wc, wa, cqg, ckvg, wuqt, wuk, wuvt, gq, gk, cos, sin, cost, sint, km, vm, convw, convb, lng, lnb, ws, bs,
      mqg, bm, wb, wo)


def _swap_rope_halves(r):
    return jnp.concatenate([r[..., HALF_ROPE:], r[..., :HALF_ROPE]], axis=-1)


def _slab(nope, rope):
    pad = jnp.zeros(rope.shape[:-1] + (LANES - QK_HEAD,), rope.dtype)
    return jnp.concatenate([nope, rope, pad], axis=-1)


def _head_gain_slab(g):
    rope = g[QK_NOPE:]
    return jnp.stack([_slab(g[:QK_NOPE], rope), _slab(jnp.zeros((QK_NOPE,), g.dtype), _swap_rope_halves(rope))])


def _prep_layer(w_in, w_uq, w_ukv, q_g, k_g, b_spatial):
    kr = w_in[:, Q_LORA + KV_LORA:OFF_CONV]
    no_nope = jnp.zeros((D_MODEL, QK_NOPE), F32)
    wa = jnp.concatenate([w_in[:, :Q_LORA + KV_LORA], _slab(no_nope, kr), _slab(no_nope, _swap_rope_halves(kr))],
                         axis=1).astype(BF16)
    uq = w_uq.reshape(Q_LORA, MLA_HEADS, QK_HEAD)
    wuqt = _slab(uq[..., :QK_NOPE], uq[..., QK_NOPE:]).reshape(Q_LORA, MLA_HEADS * LANES).T.astype(BF16)
    ukv = w_ukv.reshape(KV_LORA, MLA_HEADS, QK_NOPE + V_HEAD)
    wuk = jnp.concatenate([ukv[..., :QK_NOPE], jnp.zeros((KV_LORA, MLA_HEADS, LANES - QK_NOPE), F32)], axis=-1)
    wuk = wuk.reshape(KV_LORA, MLA_HEADS * LANES).astype(BF16)
    wuvt = ukv[..., QK_NOPE:].reshape(KV_LORA, MLA_HEADS * V_HEAD).T.astype(BF16)
    bs = jnp.repeat(b_spatial.T, SG_WIDTH // SG_GROUPS, axis=1)
    return wa, wuqt, wuk, wuvt, _head_gain_slab(q_g).T, _head_gain_slab(k_g), bs


def kernel(x, mem, positions, norm_g, w_in, cq_norm_g, ckv_norm_g, w_uq, w_ukv, mla_q_norm_g, mla_k_norm_g,
           conv_w, conv_b, sg_ln_g, sg_ln_b, w_spatial, b_spatial, mem_norm_g, w_mem_kv, mem_q_norm_g,
           mem_k_norm_g, b_merge, w_branch, w_out):
    assert x.shape[1] % TILE_ROPE == 0 and x.shape[2] == D_MODEL
    cos, sin, cost, sint = _rope_tables(positions)
    row = lambda a: a[None, :]
    for l in range(DEPTH):
        wa, wuqt, wuk, wuvt, gq, gk, bs = _prep_layer(w_in[l], w_uq[l], w_ukv[l], mla_q_norm_g[l],
                                                      mla_k_norm_g[l], b_spatial[l])
        km, vm = _mem_kv(mem, row(mem_norm_g[l]), w_mem_kv[l].astype(BF16), row(mem_k_norm_g[l]))
        wc = w_in[l][:, OFF_CONV:].astype(BF16)
        x = _layer(x, row(norm_g[l]), wc, wa, row(cq_norm_g[l]), row(ckv_norm_g[l]), wuqt, wuk, wuvt, gq, gk,
                   cos, sin, cost, sint, km, vm, conv_w[l], row(conv_b[l]), row(sg_ln_g[l]), row(sg_ln_b[l]),
                   w_spatial[l], bs, row(mem_q_norm_g[l]), b_merge[l], w_branch[l].astype(BF16),
                   w_out[l].astype(BF16))
    return x
```

```python
import functools

import jax
import jax.numpy as jnp
from jax import lax
from jax.experimental import pallas as pl
from jax.experimental.pallas import tpu as pltpu

D_MODEL = 1024
DEPTH = 2
EPS = 1e-6
MLA_HEADS = 8
QK_NOPE = 64
QK_ROPE = 32
QK_HEAD = QK_NOPE + QK_ROPE
V_HEAD = 64
Q_LORA = 256
KV_LORA = 128
ROPE_THETA = 10000.0
CONV_WIDTH = 512
SG_WIDTH = 512
SG_GROUPS = 4
SG_CHUNK = 128
MEM_HEADS = 4
MEM_HEAD_DIM = 128
N_BRANCH = 4
BRANCH_WIDTH = 512
NEG_INF = -1e30
LOG2E = 1.4426950408889634

LANES = 128
HALF_ROPE = QK_ROPE // 2
V_ROWS = V_HEAD + 16
OFF_CONV = Q_LORA + KV_LORA + QK_ROPE
C_CONV = 0
C_SG = C_CONV + 3 * CONV_WIDTH
C_MEMQ = C_SG + 2 * SG_WIDTH
C_SILU = C_MEMQ + MEM_HEADS * MEM_HEAD_DIM
C_LOGIT = C_SILU + N_BRANCH * BRANCH_WIDTH
C_END = C_LOGIT + N_BRANCH * D_MODEL

TILE_ROPE = 512
TILE_LAYER = 256
VMEM_LIMIT_BYTES = 56 * 1024 * 1024

BF16 = jnp.bfloat16
F32 = jnp.float32


def _dot(a, b):
    return jnp.dot(a, b, preferred_element_type=F32)


def _dot_nt(a, b):
    return lax.dot_general(a, b, (((1,), (1,)), ((), ())), preferred_element_type=F32)


def _rms(x, g):
    return x * lax.rsqrt(jnp.mean(x * x, axis=-1, keepdims=True) + EPS) * g


def _sigmoid(x):
    return 1.0 / (1.0 + jnp.exp(-x))


def _silu(x):
    return x * _sigmoid(x)


def _rope_table_kernel(pos_ref, invf_ref, cos_ref, sin_ref, cost_ref, sint_ref):
    pos = pos_ref[0].astype(F32)
    ang = invf_ref[...] * pos
    t = ang.shape[1]
    row = lax.broadcasted_iota(jnp.int32, ang.shape, 0)
    c = jnp.cos(ang)
    s = jnp.sin(ang)
    s = jnp.where(row < HALF_ROPE, -s, s)
    cost_ref[0] = c
    sint_ref[0] = s
    pad = jnp.zeros((LANES - QK_HEAD, t), F32)
    cos_ref[0] = jnp.concatenate([jnp.ones((QK_NOPE, t), F32), c, pad], axis=0).T
    sin_ref[0] = jnp.concatenate([jnp.zeros((QK_NOPE, t), F32), s, pad], axis=0).T


def _rope_tables(positions):
    b, s = positions.shape
    half = jnp.arange(HALF_ROPE, dtype=F32)
    inv_freq = ROPE_THETA ** (-half / HALF_ROPE)
    invf = jnp.concatenate([inv_freq, inv_freq])[:, None]
    t = TILE_ROPE
    return pl.pallas_call(
        _rope_table_kernel,
        grid=(b, s // t),
        in_specs=[pl.BlockSpec((1, 1, t), lambda i, j: (i, 0, j)),
                  pl.BlockSpec((QK_ROPE, 1), lambda i, j: (0, 0))],
        out_specs=[pl.BlockSpec((1, t, LANES), lambda i, j: (i, j, 0))] * 2
        + [pl.BlockSpec((1, QK_ROPE, t), lambda i, j: (i, 0, j))] * 2,
        out_shape=[jax.ShapeDtypeStruct((b, s, LANES), F32)] * 2
        + [jax.ShapeDtypeStruct((b, QK_ROPE, s), F32)] * 2,
        name="rope_tables",
    )(positions[:, None, :], invf)


def _mem_kv_kernel(mem_ref, mg_ref, w_ref, kg_ref, k_ref, v_ref):
    mn = _rms(mem_ref[0], mg_ref[...]).astype(BF16)
    kv = _dot(mn, w_ref[...])
    for hd in range(MEM_HEADS):
        sl = slice(hd * MEM_HEAD_DIM, (hd + 1) * MEM_HEAD_DIM)
        k_ref[0, hd] = _rms(kv[:, sl], kg_ref[...]).astype(BF16)
        off = MEM_HEADS * MEM_HEAD_DIM
        v_ref[0, hd] = kv[:, off + hd * MEM_HEAD_DIM:off + (hd + 1) * MEM_HEAD_DIM].astype(BF16)


def _mem_kv(mem, mg, w, kg):
    b, m, d = mem.shape
    const = lambda shape: pl.BlockSpec(shape, lambda i: (0,) * len(shape))
    return pl.pallas_call(
        _mem_kv_kernel,
        grid=(b,),
        in_specs=[pl.BlockSpec((1, m, d), lambda i: (i, 0, 0)), const(mg.shape), const(w.shape), const(kg.shape)],
        out_specs=[pl.BlockSpec((1, MEM_HEADS, m, MEM_HEAD_DIM), lambda i: (i, 0, 0, 0))] * 2,
        out_shape=[jax.ShapeDtypeStruct((b, MEM_HEADS, m, MEM_HEAD_DIM), BF16)] * 2,
        compiler_params=pltpu.CompilerParams(dimension_semantics=("arbitrary",),
                                             vmem_limit_bytes=VMEM_LIMIT_BYTES),
        name="mem_kv",
    )(mem, mg, w, kg)


def _layer_kernel(x_ref, ng_ref, wc_ref, wa_ref, cqg_ref, ckvg_ref, wuqt_ref, wuk_ref, wuvt_ref, gq_ref, gk_ref,
                  cos_ref, sin_ref, cost_ref, sint_ref, km_ref, vm_ref,
                  convw_ref, convb_ref, lng_ref, lnb_ref, ws_ref, bs_ref, mqg_ref,
                  bm_ref, wb_ref, wo_ref, out_ref,
                  h_ref, ysg_ref, ztail_ref, q_scr, k_scr, vt_scr, m_ref, acc_ref, mix_ref, macc_ref,
                  merged_ref):
    tm = TILE_LAYER
    i = pl.program_id(1)
    rows_i = pl.ds(pl.multiple_of(i * tm, tm), tm)
    h_ref[...] = _rms(x_ref[0], ng_ref[...]).astype(BF16)

    def proj(col, width):
        return _dot(h_ref[...], wc_ref[:, col:col + width])

    pa = _dot(h_ref[...], wa_ref[...])
    cq = _rms(pa[:, :Q_LORA], cqg_ref[...]).astype(BF16)
    ckv = _rms(pa[:, Q_LORA:Q_LORA + KV_LORA], ckvg_ref[...]).astype(BF16)
    kr = pa[:, Q_LORA + KV_LORA:Q_LORA + KV_LORA + LANES]
    krp = pa[:, Q_LORA + KV_LORA + LANES:]
    qt_all = _dot_nt(wuqt_ref[...], cq)
    k_all = _dot(ckv, wuk_ref[...])
    vt = _dot_nt(wuvt_ref[...], ckv).astype(BF16)

    qscale = QK_HEAD ** -0.5 * LOG2E
    rope_rows = slice(QK_NOPE, QK_HEAD)
    gq_nope = gq_ref[0:QK_NOPE, 0:1] * qscale
    cq_t = (gq_ref[rope_rows, 0:1] * qscale) * cost_ref[0]
    sq_t = (gq_ref[rope_rows, 1:2] * qscale) * sint_ref[0]
    for hd in range(MLA_HEADS):
        nope = qt_all[hd * LANES:hd * LANES + QK_NOPE, :]
        rope = qt_all[hd * LANES + QK_NOPE:hd * LANES + QK_HEAD, :]
        ss = jnp.sum(nope * nope, axis=0, keepdims=True) + jnp.sum(rope * rope, axis=0, keepdims=True)
        rinv = lax.rsqrt(ss / QK_HEAD + EPS)
        partner = jnp.concatenate([rope[HALF_ROPE:, :], rope[:HALF_ROPE, :]], axis=0)
        q_scr[hd, 0:QK_NOPE, :] = (nope * gq_nope * rinv).astype(BF16)
        q_scr[hd, rope_rows, :] = ((rope * cq_t + partner * sq_t) * rinv).astype(BF16)
        q_scr[hd, QK_HEAD:, :] = jnp.zeros((LANES - QK_HEAD, tm), BF16)

    gck = gk_ref[0:1, :] * cos_ref[0]
    krp_s = krp * (gk_ref[1:2, :] * sin_ref[0])
    ones_tile = jnp.where(lax.broadcasted_iota(jnp.int32, (V_ROWS - V_HEAD, tm), 0) == 0, 1.0, 0.0).astype(BF16)
    for hd in range(MLA_HEADS):
        ksl = k_all[:, hd * LANES:(hd + 1) * LANES] + kr
        ss = jnp.sum(ksl * ksl, axis=-1, keepdims=True)
        k_scr[hd, rows_i, :] = ((ksl * gck + krp_s) * lax.rsqrt(ss / QK_HEAD + EPS)).astype(BF16)
        vt_scr[hd, i, 0:V_HEAD, :] = vt[hd * V_HEAD:(hd + 1) * V_HEAD, :]
        vt_scr[hd, i, V_HEAD:, :] = ones_tile

    @pl.when(i == 0)
    def _():
        ztail_ref[...] = jnp.zeros_like(ztail_ref)

    half = CONV_WIDTH // 2
    for c in range(2):
        cs = slice(c * half, (c + 1) * half)
        bg = proj(C_CONV + c * half, half)
        cg = proj(C_CONV + CONV_WIDTH + c * half, half)
        xin = proj(C_CONV + 2 * CONV_WIDTH + c * half, half)
        z = cg * xin
        tail = ztail_ref[:, cs]
        row = lax.broadcasted_iota(jnp.int32, z.shape, 0)
        z1 = jnp.where(row == 0, tail[7:8, :], pltpu.roll(z, 1, axis=0))
        z2 = jnp.where(row == 0, tail[6:7, :], jnp.where(row == 1, tail[7:8, :], pltpu.roll(z, 2, axis=0)))
        y = convb_ref[:, cs] + convw_ref[0:1, cs] * z2
        y = y + convw_ref[1:2, cs] * z1
        y = y + convw_ref[2:3, cs] * z
        ztail_ref[:, cs] = z[tm - 8:tm, :]
        gate = proj(C_SILU + BRANCH_WIDTH + c * half, half)
        ysg_ref[1, :, cs] = (bg * y * _silu(gate)).astype(BF16)

    v = proj(C_SG + SG_WIDTH, SG_WIDTH)
    mu = jnp.mean(v, axis=-1, keepdims=True)
    vc = v - mu
    vn = (vc * lax.rsqrt(jnp.mean(vc * vc, axis=-1, keepdims=True) + EPS) * lng_ref[...] + lnb_ref[...]).astype(BF16)
    low = (lax.broadcasted_iota(jnp.int32, (SG_CHUNK, SG_CHUNK), 1)
           <= lax.broadcasted_iota(jnp.int32, (SG_CHUNK, SG_CHUNK), 0))
    gw = SG_WIDTH // SG_GROUPS
    for g in range(SG_GROUPS):
        wt = jnp.where(low, ws_ref[g], 0.0).astype(BF16)
        for c in range(tm // SG_CHUNK):
            rs = slice(c * SG_CHUNK, (c + 1) * SG_CHUNK)
            gs = slice(g * gw, (g + 1) * gw)
            mix_ref[rs, gs] = _dot(wt, vn[rs, gs]) + bs_ref[:, gs]
    u = proj(C_SG, SG_WIDTH)
    ysg_ref[2] = (u * mix_ref[...] * _silu(proj(C_SILU + 2 * BRANCH_WIDTH, BRANCH_WIDTH))).astype(BF16)

    mq = proj(C_MEMQ, MEM_HEADS * MEM_HEAD_DIM)
    outs = []
    for hd in range(MEM_HEADS):
        qh = (_rms(mq[:, hd * MEM_HEAD_DIM:(hd + 1) * MEM_HEAD_DIM], mqg_ref[...])
              * (MEM_HEAD_DIM ** -0.5)).astype(BF16)
        s = _dot_nt(qh, km_ref[0, hd])
        p = jnp.exp(s - jnp.max(s, axis=-1, keepdims=True))
        l = jnp.sum(p, axis=-1, keepdims=True)
        outs.append(_dot(p.astype(BF16), vm_ref[0, hd]) * (1.0 / l))
    yd = jnp.concatenate(outs, axis=1)
    ysg_ref[3] = (yd * _silu(proj(C_SILU + 3 * BRANCH_WIDTH, BRANCH_WIDTH))).astype(BF16)

    m_ref[...] = jnp.full(m_ref.shape, NEG_INF, F32)
    acc_ref[...] = jnp.zeros(acc_ref.shape, F32)

    def block_scores(kb):
        rows = pl.ds(pl.multiple_of(kb * tm, tm), tm)
        return [_dot(k_scr[hd, rows, :], q_scr[hd]) for hd in range(MLA_HEADS)]

    def block_update(kb, scores, mask):
        new = []
        for hd in range(MLA_HEADS):
            st = scores[hd]
            if mask:
                key = lax.broadcasted_iota(jnp.int32, (tm, tm), 0)
                qry = lax.broadcasted_iota(jnp.int32, (tm, tm), 1)
                st = jnp.where(key <= qry, st, NEG_INF)
            m_old = m_ref[hd]
            m_new = jnp.maximum(m_old, jnp.max(st, axis=0, keepdims=True))
            alpha = jnp.exp2(m_old - m_new)
            p = jnp.exp2(st - m_new)
            new.append((m_new, alpha * acc_ref[hd] + _dot(vt_scr[hd, kb], p.astype(BF16))))
        for hd, (m_new, acc_new) in enumerate(new):
            m_ref[hd] = m_new
            acc_ref[hd] = acc_new

    def kb_body(kb, carry):
        block_update(kb, block_scores(kb), False)
        return carry

    lax.fori_loop(0, i, kb_body, 0)

    diag = block_scores(i)
    nc = D_MODEL // 2

    def gated(n, c):
        zc = _dot(ysg_ref[n], wb_ref[n, :, c * nc:(c + 1) * nc])
        gate = _sigmoid(proj(C_LOGIT + n * D_MODEL + c * nc, nc) + bm_ref[n:n + 1, c * nc:(c + 1) * nc])
        return gate * zc

    for c in range(2):
        macc_ref[:, c * nc:(c + 1) * nc] = gated(1, c) + gated(2, c) + gated(3, c)
    silu0 = _silu(proj(C_SILU, BRANCH_WIDTH))
    block_update(i, diag, True)
    yat = jnp.concatenate([acc_ref[hd, 0:V_HEAD, :] * (1.0 / acc_ref[hd, V_HEAD:V_HEAD + 1, :])
                           for hd in range(MLA_HEADS)], axis=0)
    ysg_ref[0] = (yat.T * silu0).astype(BF16)

    for c in range(2):
        merged_ref[:, c * nc:(c + 1) * nc] = (macc_ref[:, c * nc:(c + 1) * nc] + gated(0, c)).astype(BF16)
    out_ref[0] = x_ref[0] + _dot(merged_ref[...], wo_ref[...])


def _layer(x, ng, wc, wa, cqg, ckvg, wuqt, wuk, wuvt, gq, gk, cos, sin, cost, sint, km, vm, convw, convb, lng, lnb,
           ws, bs, mqg, bm, wb, wo):
    b, s, d = x.shape
    tm = TILE_LAYER

    def const(shape):
        return pl.BlockSpec(shape, lambda i, j: (0,) * len(shape), pipeline_mode=pl.Buffered(1))

    def per_batch(shape):
        return pl.BlockSpec((1,) + shape[1:], lambda i, j: (i,) + (0,) * (len(shape) - 1),
                            pipeline_mode=pl.Buffered(1))

    tile = lambda width: pl.BlockSpec((1, tm, width), lambda i, j: (i, j, 0))
    tile_t = pl.BlockSpec((1, QK_ROPE, tm), lambda i, j: (i, 0, j))
    return pl.pallas_call(
        _layer_kernel,
        grid=(b, s // tm),
        in_specs=[tile(d), const(ng.shape), const(wc.shape),
                  const(wa.shape), const(cqg.shape), const(ckvg.shape), const(wuqt.shape), const(wuk.shape),
                  const(wuvt.shape), const(gq.shape), const(gk.shape), tile(LANES), tile(LANES), tile_t, tile_t,
                  per_batch(km.shape), per_batch(vm.shape),
                  const(convw.shape), const(convb.shape), const(lng.shape), const(lnb.shape),
                  const(ws.shape), const(bs.shape), const(mqg.shape), const(bm.shape),
                  const(wb.shape), const(wo.shape)],
        out_specs=tile(d),
        out_shape=jax.ShapeDtypeStruct((b, s, d), F32),
        scratch_shapes=[pltpu.VMEM((tm, d), BF16),
                        pltpu.VMEM((N_BRANCH, tm, BRANCH_WIDTH), BF16),
                        pltpu.VMEM((8, CONV_WIDTH), F32),
                        pltpu.VMEM((MLA_HEADS, LANES, tm), BF16),
                        pltpu.VMEM((MLA_HEADS, s, LANES), BF16),
                        pltpu.VMEM((MLA_HEADS, s // tm, V_ROWS, tm), BF16),
                        pltpu.VMEM((MLA_HEADS, 1, tm), F32),
                        pltpu.VMEM((MLA_HEADS, V_ROWS, tm), F32),
                        pltpu.VMEM((tm, SG_WIDTH), F32),
                        pltpu.VMEM((tm, d), F32),
                        pltpu.VMEM((tm, d), BF16)],
        compiler_params=pltpu.CompilerParams(dimension_semantics=("arbitrary", "arbitrary"),
                                             vmem_limit_bytes=VMEM_LIMIT_BYTES),
        name="hybrid_layer",
    )(x, ng, wc, wa, cqg, ckvg, wuqt, wuk, wuvt, gq, gk, cos, sin, cost, sint, km, vm, convw, convb, lng, lnb, ws, bs,
      mqg, bm, wb, wo)


def _swap_rope_halves(r):
    return jnp.concatenate([r[..., HALF_ROPE:], r[..., :HALF_ROPE]], axis=-1)


def _slab(nope, rope):
    pad = jnp.zeros(rope.shape[:-1] + (LANES - QK_HEAD,), rope.dtype)
    return jnp.concatenate([nope, rope, pad], axis=-1)


def _head_gain_slab(g):
    rope = g[QK_NOPE:]
    return jnp.stack([_slab(g[:QK_NOPE], rope), _slab(jnp.zeros((QK_NOPE,), g.dtype), _swap_rope_halves(rope))])


def _prep_layer(w_in, w_uq, w_ukv, q_g, k_g, b_spatial):
    kr = w_in[:, Q_LORA + KV_LORA:OFF_CONV]
    no_nope = jnp.zeros((D_MODEL, QK_NOPE), F32)
    wa = jnp.concatenate([w_in[:, :Q_LORA + KV_LORA], _slab(no_nope, kr), _slab(no_nope, _swap_rope_halves(kr))],
                         axis=1).astype(BF16)
    uq = w_uq.reshape(Q_LORA, MLA_HEADS, QK_HEAD)
    wuqt = _slab(uq[..., :QK_NOPE], uq[..., QK_NOPE:]).reshape(Q_LORA, MLA_HEADS * LANES).T.astype(BF16)
    ukv = w_ukv.reshape(KV_LORA, MLA_HEADS, QK_NOPE + V_HEAD)
    wuk = jnp.concatenate([ukv[..., :QK_NOPE], jnp.zeros((KV_LORA, MLA_HEADS, LANES - QK_NOPE), F32)], axis=-1)
    wuk = wuk.reshape(KV_LORA, MLA_HEADS * LANES).astype(BF16)
    wuvt = ukv[..., QK_NOPE:].reshape(KV_LORA, MLA_HEADS * V_HEAD).T.astype(BF16)
    bs = jnp.repeat(b_spatial.T, SG_WIDTH // SG_GROUPS, axis=1)
    return wa, wuqt, wuk, wuvt, _head_gain_slab(q_g).T, _head_gain_slab(k_g), bs


def kernel(x, mem, positions, norm_g, w_in, cq_norm_g, ckv_norm_g, w_uq, w_ukv, mla_q_norm_g, mla_k_norm_g,
           conv_w, conv_b, sg_ln_g, sg_ln_b, w_spatial, b_spatial, mem_norm_g, w_mem_kv, mem_q_norm_g,
           mem_k_norm_g, b_merge, w_branch, w_out):
    assert x.shape[1] % TILE_ROPE == 0 and x.shape[2] == D_MODEL
    cos, sin, cost, sint = _rope_tables(positions)
    row = lambda a: a[None, :]
    for l in range(DEPTH):
        wa, wuqt, wuk, wuvt, gq, gk, bs = _prep_layer(w_in[l], w_uq[l], w_ukv[l], mla_q_norm_g[l],
                                                      mla_k_norm_g[l], b_spatial[l])
        km, vm = _mem_kv(mem, row(mem_norm_g[l]), w_mem_kv[l].astype(BF16), row(mem_k_norm_g[l]))
        wc = w_in[l][:, OFF_CONV:].astype(BF16)
        x = _layer(x, row(norm_g[l]), wc, wa, row(cq_norm_g[l]), row(ckv_norm_g[l]), wuqt, wuk, wuvt, gq, gk,
                   cos, sin, cost, sint, km, vm, conv_w[l], row(conv_b[l]), row(sg_ln_g[l]), row(sg_ln_b[l]),
                   w_spatial[l], bs, row(mem_q_norm_g[l]), b_merge[l], w_branch[l].astype(BF16),
                   w_out[l].astype(BF16))
    return x
```

```python
import functools

import jax
import jax.numpy as jnp
from jax import lax
from jax.experimental import pallas as pl
from jax.experimental.pallas import tpu as pltpu

D_MODEL = 1024
DEPTH = 2
EPS = 1e-6
MLA_HEADS = 8
QK_NOPE = 64
QK_ROPE = 32
QK_HEAD = QK_NOPE + QK_ROPE
V_HEAD = 64
Q_LORA = 256
KV_LORA = 128
ROPE_THETA = 10000.0
CONV_WIDTH = 512
SG_WIDTH = 512
SG_GROUPS = 4
SG_CHUNK = 128
MEM_HEADS = 4
MEM_HEAD_DIM = 128
N_BRANCH = 4
BRANCH_WIDTH = 512
NEG_INF = -1e30
LOG2E = 1.4426950408889634

LANES = 128
HALF_ROPE = QK_ROPE // 2
V_ROWS = V_HEAD + 16
OFF_CONV = Q_LORA + KV_LORA + QK_ROPE
C_CONV = 0
C_SG = C_CONV + 3 * CONV_WIDTH
C_MEMQ = C_SG + 2 * SG_WIDTH
C_SILU = C_MEMQ + MEM_HEADS * MEM_HEAD_DIM
C_LOGIT = C_SILU + N_BRANCH * BRANCH_WIDTH
C_END = C_LOGIT + N_BRANCH * D_MODEL

TILE_ROPE = 512
TILE_LAYER = 256
VMEM_LIMIT_BYTES = 56 * 1024 * 1024

BF16 = jnp.bfloat16
F32 = jnp.float32


def _dot(a, b):
    return jnp.dot(a, b, preferred_element_type=F32)


def _dot_nt(a, b):
    return lax.dot_general(a, b, (((1,), (1,)), ((), ())), preferred_element_type=F32)


def _rms(x, g):
    return x * lax.rsqrt(jnp.mean(x * x, axis=-1, keepdims=True) + EPS) * g


def _sigmoid(x):
    return 1.0 / (1.0 + jnp.exp(-x))


def _silu(x):
    return x * _sigmoid(x)


def _rope_table_kernel(pos_ref, invf_ref, cos_ref, sin_ref, cost_ref, sint_ref):
    pos = pos_ref[0].astype(F32)
    ang = invf_ref[...] * pos
    t = ang.shape[1]
    row = lax.broadcasted_iota(jnp.int32, ang.shape, 0)
    c = jnp.cos(ang)
    s = jnp.sin(ang)
    s = jnp.where(row < HALF_ROPE, -s, s)
    cost_ref[0] = c
    sint_ref[0] = s
    pad = jnp.zeros((LANES - QK_HEAD, t), F32)
    cos_ref[0] = jnp.concatenate([jnp.ones((QK_NOPE, t), F32), c, pad], axis=0).T
    sin_ref[0] = jnp.concatenate([jnp.zeros((QK_NOPE, t), F32), s, pad], axis=0).T


def _rope_tables(positions):
    b, s = positions.shape
    half = jnp.arange(HALF_ROPE, dtype=F32)
    inv_freq = ROPE_THETA ** (-half / HALF_ROPE)
    invf = jnp.concatenate([inv_freq, inv_freq])[:, None]
    t = TILE_ROPE
    return pl.pallas_call(
        _rope_table_kernel,
        grid=(b, s // t),
        in_specs=[pl.BlockSpec((1, 1, t), lambda i, j: (i, 0, j)),
                  pl.BlockSpec((QK_ROPE, 1), lambda i, j: (0, 0))],
        out_specs=[pl.BlockSpec((1, t, LANES), lambda i, j: (i, j, 0))] * 2
        + [pl.BlockSpec((1, QK_ROPE, t), lambda i, j: (i, 0, j))] * 2,
        out_shape=[jax.ShapeDtypeStruct((b, s, LANES), F32)] * 2
        + [jax.ShapeDtypeStruct((b, QK_ROPE, s), F32)] * 2,
        name="rope_tables",
    )(positions[:, None, :], invf)


def _mem_kv_kernel(mem_ref, mg_ref, w_ref, kg_ref, k_ref, v_ref):
    mn = _rms(mem_ref[0], mg_ref[...]).astype(BF16)
    kv = _dot(mn, w_ref[...])
    for hd in range(MEM_HEADS):
        sl = slice(hd * MEM_HEAD_DIM, (hd + 1) * MEM_HEAD_DIM)
        k_ref[0, hd] = _rms(kv[:, sl], kg_ref[...]).astype(BF16)
        off = MEM_HEADS * MEM_HEAD_DIM
        v_ref[0, hd] = kv[:, off + hd * MEM_HEAD_DIM:off + (hd + 1) * MEM_HEAD_DIM].astype(BF16)


def _mem_kv(l, mem, mg, w, kg):
    b, m, d = mem.shape
    layer = lambda shape: pl.BlockSpec((None,) + shape[1:], lambda i: (l,) + (0,) * (len(shape) - 1))
    return pl.pallas_call(
        _mem_kv_kernel,
        grid=(b,),
        in_specs=[pl.BlockSpec((1, m, d), lambda i: (i, 0, 0)), layer(mg.shape), layer(w.shape), layer(kg.shape)],
        out_specs=[pl.BlockSpec((1, MEM_HEADS, m, MEM_HEAD_DIM), lambda i: (i, 0, 0, 0))] * 2,
        out_shape=[jax.ShapeDtypeStruct((b, MEM_HEADS, m, MEM_HEAD_DIM), BF16)] * 2,
        compiler_params=pltpu.CompilerParams(dimension_semantics=("arbitrary",),
                                             vmem_limit_bytes=VMEM_LIMIT_BYTES),
        name="mem_kv",
    )(mem, mg, w, kg)


def _layer_kernel(x_ref, ng_ref, wc_ref, wa_ref, cqg_ref, ckvg_ref, wuqt_ref, wuk_ref, wuvt_ref, gq_ref, gk_ref,
                  cos_ref, sin_ref, cost_ref, sint_ref, km_ref, vm_ref,
                  convw_ref, convb_ref, lng_ref, lnb_ref, ws_ref, bs_ref, mqg_ref,
                  bm_ref, wb_ref, wo_ref, out_ref,
                  h_ref, ysg_ref, ztail_ref, q_scr, k_scr, vt_scr, m_ref, acc_ref, mix_ref, macc_ref,
                  merged_ref):
    tm = TILE_LAYER
    i = pl.program_id(1)
    rows_i = pl.ds(pl.multiple_of(i * tm, tm), tm)
    h_ref[...] = _rms(x_ref[0], ng_ref[...]).astype(BF16)

    def proj(col, width):
        return _dot(h_ref[...], wc_ref[:, col:col + width])

    pa = _dot(h_ref[...], wa_ref[...])
    cq = _rms(pa[:, :Q_LORA], cqg_ref[...]).astype(BF16)
    ckv = _rms(pa[:, Q_LORA:Q_LORA + KV_LORA], ckvg_ref[...]).astype(BF16)
    kr = pa[:, Q_LORA + KV_LORA:Q_LORA + KV_LORA + LANES]
    krp = pa[:, Q_LORA + KV_LORA + LANES:]
    qt_all = _dot_nt(wuqt_ref[...], cq)
    k_all = _dot(ckv, wuk_ref[...])
    vt = _dot_nt(wuvt_ref[...], ckv).astype(BF16)

    qscale = QK_HEAD ** -0.5 * LOG2E
    rope_rows = slice(QK_NOPE, QK_HEAD)
    gq_nope = gq_ref[0:QK_NOPE, 0:1] * qscale
    cq_t = (gq_ref[rope_rows, 0:1] * qscale) * cost_ref[0]
    sq_t = (gq_ref[rope_rows, 1:2] * qscale) * sint_ref[0]
    for hd in range(MLA_HEADS):
        nope = qt_all[hd * LANES:hd * LANES + QK_NOPE, :]
        rope = qt_all[hd * LANES + QK_NOPE:hd * LANES + QK_HEAD, :]
        ss = jnp.sum(nope * nope, axis=0, keepdims=True) + jnp.sum(rope * rope, axis=0, keepdims=True)
        rinv = lax.rsqrt(ss / QK_HEAD + EPS)
        partner = jnp.concatenate([rope[HALF_ROPE:, :], rope[:HALF_ROPE, :]], axis=0)
        q_scr[hd, 0:QK_NOPE, :] = (nope * gq_nope * rinv).astype(BF16)
        q_scr[hd, rope_rows, :] = ((rope * cq_t + partner * sq_t) * rinv).astype(BF16)
        q_scr[hd, QK_HEAD:, :] = jnp.zeros((LANES - QK_HEAD, tm), BF16)

    gck = gk_ref[0:1, :] * cos_ref[0]
    krp_s = krp * (gk_ref[1:2, :] * sin_ref[0])
    ones_tile = jnp.where(lax.broadcasted_iota(jnp.int32, (V_ROWS - V_HEAD, tm), 0) == 0, 1.0, 0.0).astype(BF16)
    for hd in range(MLA_HEADS):
        ksl = k_all[:, hd * LANES:(hd + 1) * LANES] + kr
        ss = jnp.sum(ksl * ksl, axis=-1, keepdims=True)
        k_scr[hd, rows_i, :] = ((ksl * gck + krp_s) * lax.rsqrt(ss / QK_HEAD + EPS)).astype(BF16)
        vt_scr[hd, i, 0:V_HEAD, :] = vt[hd * V_HEAD:(hd + 1) * V_HEAD, :]
        vt_scr[hd, i, V_HEAD:, :] = ones_tile

    @pl.when(i == 0)
    def _():
        ztail_ref[...] = jnp.zeros_like(ztail_ref)

    half = CONV_WIDTH // 2
    for c in range(2):
        cs = slice(c * half, (c + 1) * half)
        bg = proj(C_CONV + c * half, half)
        cg = proj(C_CONV + CONV_WIDTH + c * half, half)
        xin = proj(C_CONV + 2 * CONV_WIDTH + c * half, half)
        z = cg * xin
        tail = ztail_ref[:, cs]
        row = lax.broadcasted_iota(jnp.int32, z.shape, 0)
        z1 = jnp.where(row == 0, tail[7:8, :], pltpu.roll(z, 1, axis=0))
        z2 = jnp.where(row == 0, tail[6:7, :], jnp.where(row == 1, tail[7:8, :], pltpu.roll(z, 2, axis=0)))
        y = convb_ref[:, cs] + convw_ref[0:1, cs] * z2
        y = y + convw_ref[1:2, cs] * z1
        y = y + convw_ref[2:3, cs] * z
        ztail_ref[:, cs] = z[tm - 8:tm, :]
        gate = proj(C_SILU + BRANCH_WIDTH + c * half, half)
        ysg_ref[1, :, cs] = (bg * y * _silu(gate)).astype(BF16)

    v = proj(C_SG + SG_WIDTH, SG_WIDTH)
    mu = jnp.mean(v, axis=-1, keepdims=True)
    vc = v - mu
    vn = (vc * lax.rsqrt(jnp.mean(vc * vc, axis=-1, keepdims=True) + EPS) * lng_ref[...] + lnb_ref[...]).astype(BF16)
    low = (lax.broadcasted_iota(jnp.int32, (SG_CHUNK, SG_CHUNK), 1)
           <= lax.broadcasted_iota(jnp.int32, (SG_CHUNK, SG_CHUNK), 0))
    gw = SG_WIDTH // SG_GROUPS
    for g in range(SG_GROUPS):
        wt = jnp.where(low, ws_ref[g], 0.0).astype(BF16)
        for c in range(tm // SG_CHUNK):
            rs = slice(c * SG_CHUNK, (c + 1) * SG_CHUNK)
            gs = slice(g * gw, (g + 1) * gw)
            mix_ref[rs, gs] = _dot(wt, vn[rs, gs]) + bs_ref[:, gs]
    u = proj(C_SG, SG_WIDTH)
    ysg_ref[2] = (u * mix_ref[...] * _silu(proj(C_SILU + 2 * BRANCH_WIDTH, BRANCH_WIDTH))).astype(BF16)

    mq = proj(C_MEMQ, MEM_HEADS * MEM_HEAD_DIM)
    outs = []
    for hd in range(MEM_HEADS):
        qh = (_rms(mq[:, hd * MEM_HEAD_DIM:(hd + 1) * MEM_HEAD_DIM], mqg_ref[...])
              * (MEM_HEAD_DIM ** -0.5)).astype(BF16)
        s = _dot_nt(qh, km_ref[0, hd])
        p = jnp.exp(s - jnp.max(s, axis=-1, keepdims=True))
        l = jnp.sum(p, axis=-1, keepdims=True)
        outs.append(_dot(p.astype(BF16), vm_ref[0, hd]) * (1.0 / l))
    yd = jnp.concatenate(outs, axis=1)
    ysg_ref[3] = (yd * _silu(proj(C_SILU + 3 * BRANCH_WIDTH, BRANCH_WIDTH))).astype(BF16)

    m_ref[...] = jnp.full(m_ref.shape, NEG_INF, F32)
    acc_ref[...] = jnp.zeros(acc_ref.shape, F32)

    def block_scores(kb):
        rows = pl.ds(pl.multiple_of(kb * tm, tm), tm)
        return [_dot(k_scr[hd, rows, :], q_scr[hd]) for hd in range(MLA_HEADS)]

    def block_update(kb, scores, mask):
        new = []
        for hd in range(MLA_HEADS):
            st = scores[hd]
            if mask:
                key = lax.broadcasted_iota(jnp.int32, (tm, tm), 0)
                qry = lax.broadcasted_iota(jnp.int32, (tm, tm), 1)
                st = jnp.where(key <= qry, st, NEG_INF)
            m_old = m_ref[hd]
            m_new = jnp.maximum(m_old, jnp.max(st, axis=0, keepdims=True))
            alpha = jnp.exp2(m_old - m_new)
            p = jnp.exp2(st - m_new)
            new.append((m_new, alpha * acc_ref[hd] + _dot(vt_scr[hd, kb], p.astype(BF16))))
        for hd, (m_new, acc_new) in enumerate(new):
            m_ref[hd] = m_new
            acc_ref[hd] = acc_new

    def kb_body(kb, carry):
        block_update(kb, block_scores(kb), False)
        return carry

    lax.fori_loop(0, i, kb_body, 0)

    diag = block_scores(i)
    nc = D_MODEL // 2

    def gated(n, c):
        zc = _dot(ysg_ref[n], wb_ref[n, :, c * nc:(c + 1) * nc])
        gate = _sigmoid(proj(C_LOGIT + n * D_MODEL + c * nc, nc) + bm_ref[n:n + 1, c * nc:(c + 1) * nc])
        return gate * zc

    for c in range(2):
        macc_ref[:, c * nc:(c + 1) * nc] = gated(1, c) + gated(2, c) + gated(3, c)
    silu0 = _silu(proj(C_SILU, BRANCH_WIDTH))
    block_update(i, diag, True)
    yat = jnp.concatenate([acc_ref[hd, 0:V_HEAD, :] * (1.0 / acc_ref[hd, V_HEAD:V_HEAD + 1, :])
                           for hd in range(MLA_HEADS)], axis=0)
    ysg_ref[0] = (yat.T * silu0).astype(BF16)

    for c in range(2):
        merged_ref[:, c * nc:(c + 1) * nc] = (macc_ref[:, c * nc:(c + 1) * nc] + gated(0, c)).astype(BF16)
    out_ref[0] = x_ref[0] + _dot(merged_ref[...], wo_ref[...])


def _layer(l, x, ng, wc, wa, cqg, ckvg, wuqt, wuk, wuvt, gq, gk, cos, sin, cost, sint, km, vm, convw, convb, lng, lnb,
           ws, bs, mqg, bm, wb, wo):
    b, s, d = x.shape
    tm = TILE_LAYER

    def const(shape):
        return pl.BlockSpec(shape, lambda i, j: (0,) * len(shape), pipeline_mode=pl.Buffered(1))

    def layer(shape):
        return pl.BlockSpec((None,) + shape[1:], lambda i, j: (l,) + (0,) * (len(shape) - 1),
                            pipeline_mode=pl.Buffered(1))

    def per_batch(shape):
        return pl.BlockSpec((1,) + shape[1:], lambda i, j: (i,) + (0,) * (len(shape) - 1),
                            pipeline_mode=pl.Buffered(1))

    tile = lambda width: pl.BlockSpec((1, tm, width), lambda i, j: (i, j, 0))
    tile_t = pl.BlockSpec((1, QK_ROPE, tm), lambda i, j: (i, 0, j))
    return pl.pallas_call(
        _layer_kernel,
        grid=(b, s // tm),
        in_specs=[tile(d), layer(ng.shape), const(wc.shape),
                  layer(wa.shape), layer(cqg.shape), layer(ckvg.shape), layer(wuqt.shape), layer(wuk.shape),
                  layer(wuvt.shape), layer(gq.shape), layer(gk.shape), tile(LANES), tile(LANES), tile_t, tile_t,
                  per_batch(km.shape), per_batch(vm.shape),
                  layer(convw.shape), layer(convb.shape), layer(lng.shape), layer(lnb.shape),
                  layer(ws.shape), layer(bs.shape), layer(mqg.shape), layer(bm.shape),
                  layer(wb.shape), layer(wo.shape)],
        out_specs=tile(d),
        out_shape=jax.ShapeDtypeStruct((b, s, d), F32),
        scratch_shapes=[pltpu.VMEM((tm, d), BF16),
                        pltpu.VMEM((N_BRANCH, tm, BRANCH_WIDTH), BF16),
                        pltpu.VMEM((8, CONV_WIDTH), F32),
                        pltpu.VMEM((MLA_HEADS, LANES, tm), BF16),
                        pltpu.VMEM((MLA_HEADS, s, LANES), BF16),
                        pltpu.VMEM((MLA_HEADS, s // tm, V_ROWS, tm), BF16),
                        pltpu.VMEM((MLA_HEADS, 1, tm), F32),
                        pltpu.VMEM((MLA_HEADS, V_ROWS, tm), F32),
                        pltpu.VMEM((tm, SG_WIDTH), F32),
                        pltpu.VMEM((tm, d), F32),
                        pltpu.VMEM((tm, d), BF16)],
        compiler_params=pltpu.CompilerParams(dimension_semantics=("arbitrary", "arbitrary"),
                                             vmem_limit_bytes=VMEM_LIMIT_BYTES),
        name="hybrid_layer",
    )(x, ng, wc, wa, cqg, ckvg, wuqt, wuk, wuvt, gq, gk, cos, sin, cost, sint, km, vm, convw, convb, lng, lnb, ws, bs,
      mqg, bm, wb, wo)


def _swap_rope_halves(r):
    return jnp.concatenate([r[..., HALF_ROPE:], r[..., :HALF_ROPE]], axis=-1)


def _slab(nope, rope):
    pad = jnp.zeros(rope.shape[:-1] + (LANES - QK_HEAD,), rope.dtype)
    return jnp.concatenate([nope, rope, pad], axis=-1)


def _head_gain_slab(g):
    rope = g[..., QK_NOPE:]
    nope = g[..., :QK_NOPE]
    return jnp.stack([_slab(nope, rope), _slab(jnp.zeros_like(nope), _swap_rope_halves(rope))], axis=-2)


def _prep(w_in, w_uq, w_ukv, q_g, k_g, b_spatial):
    n = w_in.shape[0]
    kr = w_in[..., Q_LORA + KV_LORA:OFF_CONV]
    no_nope = jnp.zeros((n, D_MODEL, QK_NOPE), F32)
    wa = jnp.concatenate([w_in[..., :Q_LORA + KV_LORA], _slab(no_nope, kr), _slab(no_nope, _swap_rope_halves(kr))],
                         axis=-1).astype(BF16)
    uq = w_uq.reshape(n, Q_LORA, MLA_HEADS, QK_HEAD)
    wuqt = _slab(uq[..., :QK_NOPE], uq[..., QK_NOPE:]).reshape(n, Q_LORA, MLA_HEADS * LANES)
    wuqt = jnp.swapaxes(wuqt, 1, 2).astype(BF16)
    ukv = w_ukv.reshape(n, KV_LORA, MLA_HEADS, QK_NOPE + V_HEAD)
    wuk = jnp.concatenate([ukv[..., :QK_NOPE], jnp.zeros((n, KV_LORA, MLA_HEADS, LANES - QK_NOPE), F32)], axis=-1)
    wuk = wuk.reshape(n, KV_LORA, MLA_HEADS * LANES).astype(BF16)
    wuvt = jnp.swapaxes(ukv[..., QK_NOPE:].reshape(n, KV_LORA, MLA_HEADS * V_HEAD), 1, 2).astype(BF16)
    bs = jnp.repeat(jnp.swapaxes(b_spatial, 1, 2), SG_WIDTH // SG_GROUPS, axis=2)
    return wa, wuqt, wuk, wuvt, jnp.swapaxes(_head_gain_slab(q_g), 1, 2), _head_gain_slab(k_g), bs


def kernel(x, mem, positions, norm_g, w_in, cq_norm_g, ckv_norm_g, w_uq, w_ukv, mla_q_norm_g, mla_k_norm_g,
           conv_w, conv_b, sg_ln_g, sg_ln_b, w_spatial, b_spatial, mem_norm_g, w_mem_kv, mem_q_norm_g,
           mem_k_norm_g, b_merge, w_branch, w_out):
    assert x.shape[1] % TILE_ROPE == 0 and x.shape[2] == D_MODEL
    cos, sin, cost, sint = _rope_tables(positions)
    row = lambda a: a[:, None, :]
    wa, wuqt, wuk, wuvt, gq, gk, bs = _prep(w_in, w_uq, w_ukv, mla_q_norm_g, mla_k_norm_g, b_spatial)
    wmkv, wb, wo = w_mem_kv.astype(BF16), w_branch.astype(BF16), w_out.astype(BF16)
    for l in range(DEPTH):
        km, vm = _mem_kv(l, mem, row(mem_norm_g), wmkv, row(mem_k_norm_g))
        wc = w_in[l][:, OFF_CONV:].astype(BF16)
        x = _layer(l, x, row(norm_g), wc, wa, row(cq_norm_g), row(ckv_norm_g), wuqt, wuk, wuvt, gq, gk,
                   cos, sin, cost, sint, km, vm, conv_w, row(conv_b), row(sg_ln_g), row(sg_ln_b),
                   w_spatial, bs, row(mem_q_norm_g), b_merge, wb, wo)
    return x
```

```python
import functools

import jax
import jax.numpy as jnp
from jax import lax
from jax.experimental import pallas as pl
from jax.experimental.pallas import tpu as pltpu

D_MODEL = 1024
DEPTH = 2
EPS = 1e-6
MLA_HEADS = 8
QK_NOPE = 64
QK_ROPE = 32
QK_HEAD = QK_NOPE + QK_ROPE
V_HEAD = 64
Q_LORA = 256
KV_LORA = 128
ROPE_THETA = 10000.0
CONV_WIDTH = 512
SG_WIDTH = 512
SG_GROUPS = 4
SG_CHUNK = 128
MEM_HEADS = 4
MEM_HEAD_DIM = 128
N_BRANCH = 4
BRANCH_WIDTH = 512
NEG_INF = -1e30
LOG2E = 1.4426950408889634

LANES = 128
HALF_ROPE = QK_ROPE // 2
V_ROWS = V_HEAD + 16
OFF_CONV = Q_LORA + KV_LORA + QK_ROPE
C_CONV = 0
C_SG = C_CONV + 3 * CONV_WIDTH
C_MEMQ = C_SG + 2 * SG_WIDTH
C_SILU = C_MEMQ + MEM_HEADS * MEM_HEAD_DIM
C_LOGIT = C_SILU + N_BRANCH * BRANCH_WIDTH
C_END = C_LOGIT + N_BRANCH * D_MODEL

TILE_ROPE = 512
TILE_LAYER = 256
VMEM_LIMIT_BYTES = 56 * 1024 * 1024

BF16 = jnp.bfloat16
F32 = jnp.float32


def _dot(a, b):
    return jnp.dot(a, b, preferred_element_type=F32)


def _dot_nt(a, b):
    return lax.dot_general(a, b, (((1,), (1,)), ((), ())), preferred_element_type=F32)


def _rms(x, g):
    return x * lax.rsqrt(jnp.mean(x * x, axis=-1, keepdims=True) + EPS) * g


def _sigmoid(x):
    return 1.0 / (1.0 + jnp.exp(-x))


def _silu(x):
    return x * _sigmoid(x)


def _rope_table_kernel(pos_ref, invf_ref, cos_ref, sin_ref, cost_ref, sint_ref):
    pos = pos_ref[0].astype(F32)
    ang = invf_ref[...] * pos
    t = ang.shape[1]
    row = lax.broadcasted_iota(jnp.int32, ang.shape, 0)
    c = jnp.cos(ang)
    s = jnp.sin(ang)
    s = jnp.where(row < HALF_ROPE, -s, s)
    cost_ref[0] = c
    sint_ref[0] = s
    pad = jnp.zeros((LANES - QK_HEAD, t), F32)
    cos_ref[0] = jnp.concatenate([jnp.ones((QK_NOPE, t), F32), c, pad], axis=0).T
    sin_ref[0] = jnp.concatenate([jnp.zeros((QK_NOPE, t), F32), s, pad], axis=0).T


def _rope_tables(positions):
    b, s = positions.shape
    half = jnp.arange(HALF_ROPE, dtype=F32)
    inv_freq = ROPE_THETA ** (-half / HALF_ROPE)
    invf = jnp.concatenate([inv_freq, inv_freq])[:, None]
    t = TILE_ROPE
    return pl.pallas_call(
        _rope_table_kernel,
        grid=(b, s // t),
        in_specs=[pl.BlockSpec((1, 1, t), lambda i, j: (i, 0, j)),
                  pl.BlockSpec((QK_ROPE, 1), lambda i, j: (0, 0))],
        out_specs=[pl.BlockSpec((1, t, LANES), lambda i, j: (i, j, 0))] * 2
        + [pl.BlockSpec((1, QK_ROPE, t), lambda i, j: (i, 0, j))] * 2,
        out_shape=[jax.ShapeDtypeStruct((b, s, LANES), F32)] * 2
        + [jax.ShapeDtypeStruct((b, QK_ROPE, s), F32)] * 2,
        name="rope_tables",
    )(positions[:, None, :], invf)


def _mem_kv_kernel(mem_ref, mg_ref, w_ref, kg_ref, k_ref, v_ref):
    mn = _rms(mem_ref[0], mg_ref[...]).astype(BF16)
    kv = _dot(mn, w_ref[...])
    for hd in range(MEM_HEADS):
        sl = slice(hd * MEM_HEAD_DIM, (hd + 1) * MEM_HEAD_DIM)
        k_ref[0, hd] = _rms(kv[:, sl], kg_ref[...]).astype(BF16)
        off = MEM_HEADS * MEM_HEAD_DIM
        v_ref[0, hd] = kv[:, off + hd * MEM_HEAD_DIM:off + (hd + 1) * MEM_HEAD_DIM].astype(BF16)


def _mem_kv(l, mem, mg, w, kg):
    b, m, d = mem.shape
    layer = lambda shape: pl.BlockSpec((None,) + shape[1:], lambda i: (l,) + (0,) * (len(shape) - 1))
    return pl.pallas_call(
        _mem_kv_kernel,
        grid=(b,),
        in_specs=[pl.BlockSpec((1, m, d), lambda i: (i, 0, 0)), layer(mg.shape), layer(w.shape), layer(kg.shape)],
        out_specs=[pl.BlockSpec((1, MEM_HEADS, m, MEM_HEAD_DIM), lambda i: (i, 0, 0, 0))] * 2,
        out_shape=[jax.ShapeDtypeStruct((b, MEM_HEADS, m, MEM_HEAD_DIM), BF16)] * 2,
        compiler_params=pltpu.CompilerParams(dimension_semantics=("arbitrary",),
                                             vmem_limit_bytes=VMEM_LIMIT_BYTES),
        name="mem_kv",
    )(mem, mg, w, kg)


def _layer_kernel(x_ref, ng_ref, wc_ref, wa_ref, cqg_ref, ckvg_ref, wuqt_ref, wuk_ref, wuvt_ref, gq_ref, gk_ref,
                  cos_ref, sin_ref, cost_ref, sint_ref, km_ref, vm_ref,
                  convw_ref, convb_ref, lng_ref, lnb_ref, ws_ref, bs_ref, mqg_ref,
                  bm_ref, wb_ref, wo_ref, out_ref,
                  h_ref, ysg_ref, ztail_ref, q_scr, k_scr, vt_scr, m_ref, acc_ref, mix_ref, macc_ref,
                  merged_ref):
    tm = TILE_LAYER
    i = pl.program_id(1)
    rows_i = pl.ds(pl.multiple_of(i * tm, tm), tm)
    h_ref[...] = _rms(x_ref[0], ng_ref[...]).astype(BF16)

    def proj(col, width):
        return _dot(h_ref[...], wc_ref[:, col:col + width])

    pa = _dot(h_ref[...], wa_ref[...])
    cq = _rms(pa[:, :Q_LORA], cqg_ref[...]).astype(BF16)
    ckv = _rms(pa[:, Q_LORA:Q_LORA + KV_LORA], ckvg_ref[...]).astype(BF16)
    kr = pa[:, Q_LORA + KV_LORA:Q_LORA + KV_LORA + LANES]
    krp = pa[:, Q_LORA + KV_LORA + LANES:]
    qt_all = _dot_nt(wuqt_ref[...], cq)
    k_all = _dot(ckv, wuk_ref[...])
    vt = _dot_nt(wuvt_ref[...], ckv).astype(BF16)

    qscale = QK_HEAD ** -0.5 * LOG2E
    rope_rows = slice(QK_NOPE, QK_HEAD)
    gq_nope = gq_ref[0:QK_NOPE, 0:1] * qscale
    cq_t = (gq_ref[rope_rows, 0:1] * qscale) * cost_ref[0]
    sq_t = (gq_ref[rope_rows, 1:2] * qscale) * sint_ref[0]
    for hd in range(MLA_HEADS):
        nope = qt_all[hd * LANES:hd * LANES + QK_NOPE, :]
        rope = qt_all[hd * LANES + QK_NOPE:hd * LANES + QK_HEAD, :]
        ss = jnp.sum(nope * nope, axis=0, keepdims=True) + jnp.sum(rope * rope, axis=0, keepdims=True)
        rinv = lax.rsqrt(ss / QK_HEAD + EPS)
        partner = jnp.concatenate([rope[HALF_ROPE:, :], rope[:HALF_ROPE, :]], axis=0)
        q_scr[hd, 0:QK_NOPE, :] = (nope * gq_nope * rinv).astype(BF16)
        q_scr[hd, rope_rows, :] = ((rope * cq_t + partner * sq_t) * rinv).astype(BF16)
        q_scr[hd, QK_HEAD:, :] = jnp.zeros((LANES - QK_HEAD, tm), BF16)

    gck = gk_ref[0:1, :] * cos_ref[0]
    krp_s = krp * (gk_ref[1:2, :] * sin_ref[0])
    ones_tile = jnp.where(lax.broadcasted_iota(jnp.int32, (V_ROWS - V_HEAD, tm), 0) == 0, 1.0, 0.0).astype(BF16)
    for hd in range(MLA_HEADS):
        ksl = k_all[:, hd * LANES:(hd + 1) * LANES] + kr
        ss = jnp.sum(ksl * ksl, axis=-1, keepdims=True)
        k_scr[hd, rows_i, :] = ((ksl * gck + krp_s) * lax.rsqrt(ss / QK_HEAD + EPS)).astype(BF16)
        vt_scr[hd, i, 0:V_HEAD, :] = vt[hd * V_HEAD:(hd + 1) * V_HEAD, :]
        vt_scr[hd, i, V_HEAD:, :] = ones_tile

    @pl.when(i == 0)
    def _():
        ztail_ref[...] = jnp.zeros_like(ztail_ref)

    half = CONV_WIDTH // 2
    for c in range(2):
        cs = slice(c * half, (c + 1) * half)
        bg = proj(C_CONV + c * half, half)
        cg = proj(C_CONV + CONV_WIDTH + c * half, half)
        xin = proj(C_CONV + 2 * CONV_WIDTH + c * half, half)
        z = cg * xin
        tail = ztail_ref[:, cs]
        row = lax.broadcasted_iota(jnp.int32, z.shape, 0)
        z1 = jnp.where(row == 0, tail[7:8, :], pltpu.roll(z, 1, axis=0))
        z2 = jnp.where(row == 0, tail[6:7, :], jnp.where(row == 1, tail[7:8, :], pltpu.roll(z, 2, axis=0)))
        y = convb_ref[:, cs] + convw_ref[0:1, cs] * z2
        y = y + convw_ref[1:2, cs] * z1
        y = y + convw_ref[2:3, cs] * z
        ztail_ref[:, cs] = z[tm - 8:tm, :]
        gate = proj(C_SILU + BRANCH_WIDTH + c * half, half)
        ysg_ref[1, :, cs] = (bg * y * _silu(gate)).astype(BF16)

    v = proj(C_SG + SG_WIDTH, SG_WIDTH)
    mu = jnp.mean(v, axis=-1, keepdims=True)
    vc = v - mu
    vn = (vc * lax.rsqrt(jnp.mean(vc * vc, axis=-1, keepdims=True) + EPS) * lng_ref[...] + lnb_ref[...]).astype(BF16)
    low = (lax.broadcasted_iota(jnp.int32, (SG_CHUNK, SG_CHUNK), 1)
           <= lax.broadcasted_iota(jnp.int32, (SG_CHUNK, SG_CHUNK), 0))
    gw = SG_WIDTH // SG_GROUPS
    for g in range(SG_GROUPS):
        wt = jnp.where(low, ws_ref[g], 0.0).astype(BF16)
        for c in range(tm // SG_CHUNK):
            rs = slice(c * SG_CHUNK, (c + 1) * SG_CHUNK)
            gs = slice(g * gw, (g + 1) * gw)
            mix_ref[rs, gs] = _dot(wt, vn[rs, gs]) + bs_ref[:, gs]
    u = proj(C_SG, SG_WIDTH)
    ysg_ref[2] = (u * mix_ref[...] * _silu(proj(C_SILU + 2 * BRANCH_WIDTH, BRANCH_WIDTH))).astype(BF16)

    mq = proj(C_MEMQ, MEM_HEADS * MEM_HEAD_DIM)
    outs = []
    for hd in range(MEM_HEADS):
        qh = (_rms(mq[:, hd * MEM_HEAD_DIM:(hd + 1) * MEM_HEAD_DIM], mqg_ref[...])
              * (MEM_HEAD_DIM ** -0.5)).astype(BF16)
        s = _dot_nt(qh, km_ref[0, hd])
        p = jnp.exp(s - jnp.max(s, axis=-1, keepdims=True))
        l = jnp.sum(p, axis=-1, keepdims=True)
        outs.append(_dot(p.astype(BF16), vm_ref[0, hd]) * (1.0 / l))
    yd = jnp.concatenate(outs, axis=1)
    ysg_ref[3] = (yd * _silu(proj(C_SILU + 3 * BRANCH_WIDTH, BRANCH_WIDTH))).astype(BF16)

    m_ref[...] = jnp.full(m_ref.shape, NEG_INF, F32)
    acc_ref[...] = jnp.zeros(acc_ref.shape, F32)

    def block_scores(kb):
        rows = pl.ds(pl.multiple_of(kb * tm, tm), tm)
        return [_dot(k_scr[hd, rows, :], q_scr[hd]) for hd in range(MLA_HEADS)]

    def block_update(kb, scores, mask):
        new = []
        for hd in range(MLA_HEADS):
            st = scores[hd]
            if mask:
                key = lax.broadcasted_iota(jnp.int32, (tm, tm), 0)
                qry = lax.broadcasted_iota(jnp.int32, (tm, tm), 1)
                st = jnp.where(key <= qry, st, NEG_INF)
            m_old = m_ref[hd]
            m_new = jnp.maximum(m_old, jnp.max(st, axis=0, keepdims=True))
            alpha = jnp.exp2(m_old - m_new)
            p = jnp.exp2(st - m_new)
            new.append((m_new, alpha * acc_ref[hd] + _dot(vt_scr[hd, kb], p.astype(BF16))))
        for hd, (m_new, acc_new) in enumerate(new):
            m_ref[hd] = m_new
            acc_ref[hd] = acc_new

    def kb_body(kb, carry):
        block_update(kb, block_scores(kb), False)
        return carry

    lax.fori_loop(0, i, kb_body, 0)

    diag = block_scores(i)
    nc = D_MODEL // 2

    def gated(n, c):
        zc = _dot(ysg_ref[n], wb_ref[n, :, c * nc:(c + 1) * nc])
        gate = _sigmoid(proj(C_LOGIT + n * D_MODEL + c * nc, nc) + bm_ref[n:n + 1, c * nc:(c + 1) * nc])
        return gate * zc

    for c in range(2):
        macc_ref[:, c * nc:(c + 1) * nc] = gated(1, c) + gated(2, c) + gated(3, c)
    silu0 = _silu(proj(C_SILU, BRANCH_WIDTH))
    block_update(i, diag, True)
    yat = jnp.concatenate([acc_ref[hd, 0:V_HEAD, :] * (1.0 / acc_ref[hd, V_HEAD:V_HEAD + 1, :])
                           for hd in range(MLA_HEADS)], axis=0)
    ysg_ref[0] = (yat.T * silu0).astype(BF16)

    for c in range(2):
        merged_ref[:, c * nc:(c + 1) * nc] = (macc_ref[:, c * nc:(c + 1) * nc] + gated(0, c)).astype(BF16)
    out_ref[0] = x_ref[0] + _dot(merged_ref[...], wo_ref[...])


def _layer(l, x, ng, wc, wa, cqg, ckvg, wuqt, wuk, wuvt, gq, gk, cos, sin, cost, sint, km, vm, convw, convb, lng, lnb,
           ws, bs, mqg, bm, wb, wo):
    b, s, d = x.shape
    tm = TILE_LAYER

    def const(shape):
        return pl.BlockSpec(shape, lambda i, j: (0,) * len(shape), pipeline_mode=pl.Buffered(1))

    def layer(shape):
        return pl.BlockSpec((None,) + shape[1:], lambda i, j: (l,) + (0,) * (len(shape) - 1),
                            pipeline_mode=pl.Buffered(1))

    def per_batch(shape):
        return pl.BlockSpec((1,) + shape[1:], lambda i, j: (i,) + (0,) * (len(shape) - 1),
                            pipeline_mode=pl.Buffered(1))

    tile = lambda width: pl.BlockSpec((1, tm, width), lambda i, j: (i, j, 0))
    tile_t = pl.BlockSpec((1, QK_ROPE, tm), lambda i, j: (i, 0, j))
    return pl.pallas_call(
        _layer_kernel,
        grid=(b, s // tm),
        in_specs=[tile(d), layer(ng.shape), const(wc.shape),
                  const(wa.shape), layer(cqg.shape), layer(ckvg.shape), layer(wuqt.shape), layer(wuk.shape),
                  layer(wuvt.shape), layer(gq.shape), layer(gk.shape), tile(LANES), tile(LANES), tile_t, tile_t,
                  per_batch(km.shape), per_batch(vm.shape),
                  layer(convw.shape), layer(convb.shape), layer(lng.shape), layer(lnb.shape),
                  layer(ws.shape), layer(bs.shape), layer(mqg.shape), layer(bm.shape),
                  layer(wb.shape), layer(wo.shape)],
        out_specs=tile(d),
        out_shape=jax.ShapeDtypeStruct((b, s, d), F32),
        scratch_shapes=[pltpu.VMEM((tm, d), BF16),
                        pltpu.VMEM((N_BRANCH, tm, BRANCH_WIDTH), BF16),
                        pltpu.VMEM((8, CONV_WIDTH), F32),
                        pltpu.VMEM((MLA_HEADS, LANES, tm), BF16),
                        pltpu.VMEM((MLA_HEADS, s, LANES), BF16),
                        pltpu.VMEM((MLA_HEADS, s // tm, V_ROWS, tm), BF16),
                        pltpu.VMEM((MLA_HEADS, 1, tm), F32),
                        pltpu.VMEM((MLA_HEADS, V_ROWS, tm), F32),
                        pltpu.VMEM((tm, SG_WIDTH), F32),
                        pltpu.VMEM((tm, d), F32),
                        pltpu.VMEM((tm, d), BF16)],
        compiler_params=pltpu.CompilerParams(dimension_semantics=("arbitrary", "arbitrary"),
                                             vmem_limit_bytes=VMEM_LIMIT_BYTES),
        name="hybrid_layer",
    )(x, ng, wc, wa, cqg, ckvg, wuqt, wuk, wuvt, gq, gk, cos, sin, cost, sint, km, vm, convw, convb, lng, lnb, ws, bs,
      mqg, bm, wb, wo)


def _swap_rope_halves(r):
    return jnp.concatenate([r[..., HALF_ROPE:], r[..., :HALF_ROPE]], axis=-1)


def _slab(nope, rope):
    pad = jnp.zeros(rope.shape[:-1] + (LANES - QK_HEAD,), rope.dtype)
    return jnp.concatenate([nope, rope, pad], axis=-1)


def _head_gain_slab(g):
    rope = g[..., QK_NOPE:]
    nope = g[..., :QK_NOPE]
    return jnp.stack([_slab(nope, rope), _slab(jnp.zeros_like(nope), _swap_rope_halves(rope))], axis=-2)


def _latent_weights(w_in):
    kr = w_in[:, Q_LORA + KV_LORA:OFF_CONV]
    no_nope = jnp.zeros((D_MODEL, QK_NOPE), F32)
    return jnp.concatenate([w_in[:, :Q_LORA + KV_LORA], _slab(no_nope, kr), _slab(no_nope, _swap_rope_halves(kr))],
                           axis=-1).astype(BF16)


def _prep(w_uq, w_ukv, q_g, k_g, b_spatial):
    n = w_uq.shape[0]
    uq = w_uq.reshape(n, Q_LORA, MLA_HEADS, QK_HEAD)
    wuqt = _slab(uq[..., :QK_NOPE], uq[..., QK_NOPE:]).reshape(n, Q_LORA, MLA_HEADS * LANES)
    wuqt = jnp.swapaxes(wuqt, 1, 2).astype(BF16)
    ukv = w_ukv.reshape(n, KV_LORA, MLA_HEADS, QK_NOPE + V_HEAD)
    wuk = jnp.concatenate([ukv[..., :QK_NOPE], jnp.zeros((n, KV_LORA, MLA_HEADS, LANES - QK_NOPE), F32)], axis=-1)
    wuk = wuk.reshape(n, KV_LORA, MLA_HEADS * LANES).astype(BF16)
    wuvt = jnp.swapaxes(ukv[..., QK_NOPE:].reshape(n, KV_LORA, MLA_HEADS * V_HEAD), 1, 2).astype(BF16)
    bs = jnp.repeat(jnp.swapaxes(b_spatial, 1, 2), SG_WIDTH // SG_GROUPS, axis=2)
    return wuqt, wuk, wuvt, jnp.swapaxes(_head_gain_slab(q_g), 1, 2), _head_gain_slab(k_g), bs


def kernel(x, mem, positions, norm_g, w_in, cq_norm_g, ckv_norm_g, w_uq, w_ukv, mla_q_norm_g, mla_k_norm_g,
           conv_w, conv_b, sg_ln_g, sg_ln_b, w_spatial, b_spatial, mem_norm_g, w_mem_kv, mem_q_norm_g,
           mem_k_norm_g, b_merge, w_branch, w_out):
    assert x.shape[1] % TILE_ROPE == 0 and x.shape[2] == D_MODEL
    cos, sin, cost, sint = _rope_tables(positions)
    row = lambda a: a[:, None, :]
    wuqt, wuk, wuvt, gq, gk, bs = _prep(w_uq, w_ukv, mla_q_norm_g, mla_k_norm_g, b_spatial)
    wmkv, wb, wo = w_mem_kv.astype(BF16), w_branch.astype(BF16), w_out.astype(BF16)
    for l in range(DEPTH):
        km, vm = _mem_kv(l, mem, row(mem_norm_g), wmkv, row(mem_k_norm_g))
        wa = _latent_weights(w_in[l])
        wc = w_in[l][:, OFF_CONV:].astype(BF16)
        x = _layer(l, x, row(norm_g), wc, wa, row(cq_norm_g), row(ckv_norm_g), wuqt, wuk, wuvt, gq, gk,
                   cos, sin, cost, sint, km, vm, conv_w, row(conv_b), row(sg_ln_g), row(sg_ln_b),
                   w_spatial, bs, row(mem_q_norm_g), b_merge, wb, wo)
    return x
```

```python
import functools

import jax
import jax.numpy as jnp
from jax import lax
from jax.experimental import pallas as pl
from jax.experimental.pallas import tpu as pltpu

D_MODEL = 1024
DEPTH = 2
EPS = 1e-6
MLA_HEADS = 8
QK_NOPE = 64
QK_ROPE = 32
QK_HEAD = QK_NOPE + QK_ROPE
V_HEAD = 64
Q_LORA = 256
KV_LORA = 128
ROPE_THETA = 10000.0
CONV_WIDTH = 512
SG_WIDTH = 512
SG_GROUPS = 4
SG_CHUNK = 128
MEM_HEADS = 4
MEM_HEAD_DIM = 128
N_BRANCH = 4
BRANCH_WIDTH = 512
NEG_INF = -1e30
LOG2E = 1.4426950408889634

LANES = 128
HALF_ROPE = QK_ROPE // 2
V_ROWS = V_HEAD + 16
OFF_CONV = Q_LORA + KV_LORA + QK_ROPE
C_CONV = 0
C_SG = C_CONV + 3 * CONV_WIDTH
C_MEMQ = C_SG + 2 * SG_WIDTH
C_SILU = C_MEMQ + MEM_HEADS * MEM_HEAD_DIM
C_LOGIT = C_SILU + N_BRANCH * BRANCH_WIDTH
C_END = C_LOGIT + N_BRANCH * D_MODEL

TILE_ROPE = 512
TILE_LAYER = 256
VMEM_LIMIT_BYTES = 56 * 1024 * 1024

BF16 = jnp.bfloat16
F32 = jnp.float32


def _dot(a, b):
    return jnp.dot(a, b, preferred_element_type=F32)


def _dot_nt(a, b):
    return lax.dot_general(a, b, (((1,), (1,)), ((), ())), preferred_element_type=F32)


def _rms(x, g):
    return x * lax.rsqrt(jnp.mean(x * x, axis=-1, keepdims=True) + EPS) * g


def _sigmoid(x):
    return 0.5 * jnp.tanh(0.5 * x) + 0.5


def _silu(x):
    return x * _sigmoid(x)


def _rope_table_kernel(pos_ref, invf_ref, cos_ref, sin_ref, cost_ref, sint_ref):
    pos = pos_ref[0].astype(F32)
    ang = invf_ref[...] * pos
    t = ang.shape[1]
    row = lax.broadcasted_iota(jnp.int32, ang.shape, 0)
    c = jnp.cos(ang)
    s = jnp.sin(ang)
    s = jnp.where(row < HALF_ROPE, -s, s)
    cost_ref[0] = c
    sint_ref[0] = s
    pad = jnp.zeros((LANES - QK_HEAD, t), F32)
    cos_ref[0] = jnp.concatenate([jnp.ones((QK_NOPE, t), F32), c, pad], axis=0).T
    sin_ref[0] = jnp.concatenate([jnp.zeros((QK_NOPE, t), F32), s, pad], axis=0).T


def _rope_tables(positions):
    b, s = positions.shape
    half = jnp.arange(HALF_ROPE, dtype=F32)
    inv_freq = ROPE_THETA ** (-half / HALF_ROPE)
    invf = jnp.concatenate([inv_freq, inv_freq])[:, None]
    t = TILE_ROPE
    return pl.pallas_call(
        _rope_table_kernel,
        grid=(b, s // t),
        in_specs=[pl.BlockSpec((1, 1, t), lambda i, j: (i, 0, j)),
                  pl.BlockSpec((QK_ROPE, 1), lambda i, j: (0, 0))],
        out_specs=[pl.BlockSpec((1, t, LANES), lambda i, j: (i, j, 0))] * 2
        + [pl.BlockSpec((1, QK_ROPE, t), lambda i, j: (i, 0, j))] * 2,
        out_shape=[jax.ShapeDtypeStruct((b, s, LANES), F32)] * 2
        + [jax.ShapeDtypeStruct((b, QK_ROPE, s), F32)] * 2,
        name="rope_tables",
    )(positions[:, None, :], invf)


def _mem_kv_kernel(mem_ref, mg_ref, w_ref, kg_ref, k_ref, v_ref):
    mn = _rms(mem_ref[0], mg_ref[...]).astype(BF16)
    kv = _dot(mn, w_ref[...])
    for hd in range(MEM_HEADS):
        sl = slice(hd * MEM_HEAD_DIM, (hd + 1) * MEM_HEAD_DIM)
        k_ref[0, hd] = _rms(kv[:, sl], kg_ref[...]).astype(BF16)
        off = MEM_HEADS * MEM_HEAD_DIM
        v_ref[0, hd] = kv[:, off + hd * MEM_HEAD_DIM:off + (hd + 1) * MEM_HEAD_DIM].astype(BF16)


def _mem_kv(l, mem, mg, w, kg):
    b, m, d = mem.shape
    layer = lambda shape: pl.BlockSpec((None,) + shape[1:], lambda i: (l,) + (0,) * (len(shape) - 1))
    return pl.pallas_call(
        _mem_kv_kernel,
        grid=(b,),
        in_specs=[pl.BlockSpec((1, m, d), lambda i: (i, 0, 0)), layer(mg.shape), layer(w.shape), layer(kg.shape)],
        out_specs=[pl.BlockSpec((1, MEM_HEADS, m, MEM_HEAD_DIM), lambda i: (i, 0, 0, 0))] * 2,
        out_shape=[jax.ShapeDtypeStruct((b, MEM_HEADS, m, MEM_HEAD_DIM), BF16)] * 2,
        compiler_params=pltpu.CompilerParams(dimension_semantics=("arbitrary",),
                                             vmem_limit_bytes=VMEM_LIMIT_BYTES),
        name="mem_kv",
    )(mem, mg, w, kg)


def _layer_kernel(x_ref, ng_ref, wc_ref, wa_ref, cqg_ref, ckvg_ref, wuqt_ref, wuk_ref, wuvt_ref, gq_ref, gk_ref,
                  cos_ref, sin_ref, cost_ref, sint_ref, km_ref, vm_ref,
                  convw_ref, convb_ref, lng_ref, lnb_ref, ws_ref, bs_ref, mqg_ref,
                  bm_ref, wb_ref, wo_ref, out_ref,
                  h_ref, ysg_ref, ztail_ref, q_scr, k_scr, vt_scr, m_ref, acc_ref, mix_ref, macc_ref,
                  merged_ref):
    tm = TILE_LAYER
    i = pl.program_id(1)
    rows_i = pl.ds(pl.multiple_of(i * tm, tm), tm)
    h_ref[...] = _rms(x_ref[0], ng_ref[...]).astype(BF16)

    def proj(col, width):
        return _dot(h_ref[...], wc_ref[:, col:col + width])

    pa = _dot(h_ref[...], wa_ref[...])
    cq = _rms(pa[:, :Q_LORA], cqg_ref[...]).astype(BF16)
    ckv = _rms(pa[:, Q_LORA:Q_LORA + KV_LORA], ckvg_ref[...]).astype(BF16)
    kr = pa[:, Q_LORA + KV_LORA:Q_LORA + KV_LORA + LANES]
    krp = pa[:, Q_LORA + KV_LORA + LANES:]
    qt_all = _dot_nt(wuqt_ref[...], cq)
    k_all = _dot(ckv, wuk_ref[...])
    vt = _dot_nt(wuvt_ref[...], ckv).astype(BF16)

    qscale = QK_HEAD ** -0.5 * LOG2E
    rope_rows = slice(QK_NOPE, QK_HEAD)
    gq_nope = gq_ref[0:QK_NOPE, 0:1] * qscale
    cq_t = (gq_ref[rope_rows, 0:1] * qscale) * cost_ref[0]
    sq_t = (gq_ref[rope_rows, 1:2] * qscale) * sint_ref[0]
    for hd in range(MLA_HEADS):
        nope = qt_all[hd * LANES:hd * LANES + QK_NOPE, :]
        rope = qt_all[hd * LANES + QK_NOPE:hd * LANES + QK_HEAD, :]
        ss = jnp.sum(nope * nope, axis=0, keepdims=True) + jnp.sum(rope * rope, axis=0, keepdims=True)
        rinv = lax.rsqrt(ss / QK_HEAD + EPS)
        partner = jnp.concatenate([rope[HALF_ROPE:, :], rope[:HALF_ROPE, :]], axis=0)
        q_scr[hd, 0:QK_NOPE, :] = (nope * gq_nope * rinv).astype(BF16)
        q_scr[hd, rope_rows, :] = ((rope * cq_t + partner * sq_t) * rinv).astype(BF16)
        q_scr[hd, QK_HEAD:, :] = jnp.zeros((LANES - QK_HEAD, tm), BF16)

    gck = gk_ref[0:1, :] * cos_ref[0]
    krp_s = krp * (gk_ref[1:2, :] * sin_ref[0])
    ones_tile = jnp.where(lax.broadcasted_iota(jnp.int32, (V_ROWS - V_HEAD, tm), 0) == 0, 1.0, 0.0).astype(BF16)
    for hd in range(MLA_HEADS):
        ksl = k_all[:, hd * LANES:(hd + 1) * LANES] + kr
        ss = jnp.sum(ksl * ksl, axis=-1, keepdims=True)
        k_scr[hd, rows_i, :] = ((ksl * gck + krp_s) * lax.rsqrt(ss / QK_HEAD + EPS)).astype(BF16)
        vt_scr[hd, i, 0:V_HEAD, :] = vt[hd * V_HEAD:(hd + 1) * V_HEAD, :]
        vt_scr[hd, i, V_HEAD:, :] = ones_tile

    @pl.when(i == 0)
    def _():
        ztail_ref[...] = jnp.zeros_like(ztail_ref)

    half = CONV_WIDTH // 2
    for c in range(2):
        cs = slice(c * half, (c + 1) * half)
        bg = proj(C_CONV + c * half, half)
        cg = proj(C_CONV + CONV_WIDTH + c * half, half)
        xin = proj(C_CONV + 2 * CONV_WIDTH + c * half, half)
        z = cg * xin
        tail = ztail_ref[:, cs]
        row = lax.broadcasted_iota(jnp.int32, z.shape, 0)
        z1 = jnp.where(row == 0, tail[7:8, :], pltpu.roll(z, 1, axis=0))
        z2 = jnp.where(row == 0, tail[6:7, :], jnp.where(row == 1, tail[7:8, :], pltpu.roll(z, 2, axis=0)))
        y = convb_ref[:, cs] + convw_ref[0:1, cs] * z2
        y = y + convw_ref[1:2, cs] * z1
        y = y + convw_ref[2:3, cs] * z
        ztail_ref[:, cs] = z[tm - 8:tm, :]
        gate = proj(C_SILU + BRANCH_WIDTH + c * half, half)
        ysg_ref[1, :, cs] = (bg * y * _silu(gate)).astype(BF16)

    v = proj(C_SG + SG_WIDTH, SG_WIDTH)
    mu = jnp.mean(v, axis=-1, keepdims=True)
    vc = v - mu
    vn = (vc * lax.rsqrt(jnp.mean(vc * vc, axis=-1, keepdims=True) + EPS) * lng_ref[...] + lnb_ref[...]).astype(BF16)
    low = (lax.broadcasted_iota(jnp.int32, (SG_CHUNK, SG_CHUNK), 1)
           <= lax.broadcasted_iota(jnp.int32, (SG_CHUNK, SG_CHUNK), 0))
    gw = SG_WIDTH // SG_GROUPS
    for g in range(SG_GROUPS):
        wt = jnp.where(low, ws_ref[g], 0.0).astype(BF16)
        for c in range(tm // SG_CHUNK):
            rs = slice(c * SG_CHUNK, (c + 1) * SG_CHUNK)
            gs = slice(g * gw, (g + 1) * gw)
            mix_ref[rs, gs] = _dot(wt, vn[rs, gs]) + bs_ref[:, gs]
    u = proj(C_SG, SG_WIDTH)
    ysg_ref[2] = (u * mix_ref[...] * _silu(proj(C_SILU + 2 * BRANCH_WIDTH, BRANCH_WIDTH))).astype(BF16)

    mq = proj(C_MEMQ, MEM_HEADS * MEM_HEAD_DIM)
    outs = []
    for hd in range(MEM_HEADS):
        qh = (_rms(mq[:, hd * MEM_HEAD_DIM:(hd + 1) * MEM_HEAD_DIM], mqg_ref[...])
              * (MEM_HEAD_DIM ** -0.5)).astype(BF16)
        s = _dot_nt(qh, km_ref[0, hd])
        p = jnp.exp(s - jnp.max(s, axis=-1, keepdims=True))
        l = jnp.sum(p, axis=-1, keepdims=True)
        outs.append(_dot(p.astype(BF16), vm_ref[0, hd]) * (1.0 / l))
    yd = jnp.concatenate(outs, axis=1)
    ysg_ref[3] = (yd * _silu(proj(C_SILU + 3 * BRANCH_WIDTH, BRANCH_WIDTH))).astype(BF16)

    m_ref[...] = jnp.full(m_ref.shape, NEG_INF, F32)
    acc_ref[...] = jnp.zeros(acc_ref.shape, F32)

    def block_scores(kb):
        rows = pl.ds(pl.multiple_of(kb * tm, tm), tm)
        return [_dot(k_scr[hd, rows, :], q_scr[hd]) for hd in range(MLA_HEADS)]

    def block_update(kb, scores, mask):
        new = []
        for hd in range(MLA_HEADS):
            st = scores[hd]
            if mask:
                key = lax.broadcasted_iota(jnp.int32, (tm, tm), 0)
                qry = lax.broadcasted_iota(jnp.int32, (tm, tm), 1)
                st = jnp.where(key <= qry, st, NEG_INF)
            m_old = m_ref[hd]
            m_new = jnp.maximum(m_old, jnp.max(st, axis=0, keepdims=True))
            alpha = jnp.exp2(m_old - m_new)
            p = jnp.exp2(st - m_new)
            new.append((m_new, alpha * acc_ref[hd] + _dot(vt_scr[hd, kb], p.astype(BF16))))
        for hd, (m_new, acc_new) in enumerate(new):
            m_ref[hd] = m_new
            acc_ref[hd] = acc_new

    def kb_body(kb, carry):
        block_update(kb, block_scores(kb), False)
        return carry

    lax.fori_loop(0, i, kb_body, 0)

    diag = block_scores(i)
    nc = D_MODEL // 2

    def gated(n, c):
        zc = _dot(ysg_ref[n], wb_ref[n, :, c * nc:(c + 1) * nc])
        gate = _sigmoid(proj(C_LOGIT + n * D_MODEL + c * nc, nc) + bm_ref[n:n + 1, c * nc:(c + 1) * nc])
        return gate * zc

    for c in range(2):
        macc_ref[:, c * nc:(c + 1) * nc] = gated(1, c) + gated(2, c) + gated(3, c)
    silu0 = _silu(proj(C_SILU, BRANCH_WIDTH))
    block_update(i, diag, True)
    yat = jnp.concatenate([acc_ref[hd, 0:V_HEAD, :] * (1.0 / acc_ref[hd, V_HEAD:V_HEAD + 1, :])
                           for hd in range(MLA_HEADS)], axis=0)
    ysg_ref[0] = (yat.T * silu0).astype(BF16)

    for c in range(2):
        merged_ref[:, c * nc:(c + 1) * nc] = (macc_ref[:, c * nc:(c + 1) * nc] + gated(0, c)).astype(BF16)
    out_ref[0] = x_ref[0] + _dot(merged_ref[...], wo_ref[...])


def _layer(l, x, ng, wc, wa, cqg, ckvg, wuqt, wuk, wuvt, gq, gk, cos, sin, cost, sint, km, vm, convw, convb, lng, lnb,
           ws, bs, mqg, bm, wb, wo):
    b, s, d = x.shape
    tm = TILE_LAYER

    def const(shape):
        return pl.BlockSpec(shape, lambda i, j: (0,) * len(shape), pipeline_mode=pl.Buffered(1))

    def layer(shape):
        return pl.BlockSpec((None,) + shape[1:], lambda i, j: (l,) + (0,) * (len(shape) - 1),
                            pipeline_mode=pl.Buffered(1))

    def per_batch(shape):
        return pl.BlockSpec((1,) + shape[1:], lambda i, j: (i,) + (0,) * (len(shape) - 1),
                            pipeline_mode=pl.Buffered(1))

    tile = lambda width: pl.BlockSpec((1, tm, width), lambda i, j: (i, j, 0))
    tile_t = pl.BlockSpec((1, QK_ROPE, tm), lambda i, j: (i, 0, j))
    return pl.pallas_call(
        _layer_kernel,
        grid=(b, s // tm),
        in_specs=[tile(d), layer(ng.shape), const(wc.shape),
                  const(wa.shape), layer(cqg.shape), layer(ckvg.shape), layer(wuqt.shape), layer(wuk.shape),
                  layer(wuvt.shape), layer(gq.shape), layer(gk.shape), tile(LANES), tile(LANES), tile_t, tile_t,
                  per_batch(km.shape), per_batch(vm.shape),
                  layer(convw.shape), layer(convb.shape), layer(lng.shape), layer(lnb.shape),
                  layer(ws.shape), layer(bs.shape), layer(mqg.shape), layer(bm.shape),
                  layer(wb.shape), layer(wo.shape)],
        out_specs=tile(d),
        out_shape=jax.ShapeDtypeStruct((b, s, d), F32),
        scratch_shapes=[pltpu.VMEM((tm, d), BF16),
                        pltpu.VMEM((N_BRANCH, tm, BRANCH_WIDTH), BF16),
                        pltpu.VMEM((8, CONV_WIDTH), F32),
                        pltpu.VMEM((MLA_HEADS, LANES, tm), BF16),
                        pltpu.VMEM((MLA_HEADS, s, LANES), BF16),
                        pltpu.VMEM((MLA_HEADS, s // tm, V_ROWS, tm), BF16),
                        pltpu.VMEM((MLA_HEADS, 1, tm), F32),
                        pltpu.VMEM((MLA_HEADS, V_ROWS, tm), F32),
                        pltpu.VMEM((tm, SG_WIDTH), F32),
                        pltpu.VMEM((tm, d), F32),
                        pltpu.VMEM((tm, d), BF16)],
        compiler_params=pltpu.CompilerParams(dimension_semantics=("arbitrary", "arbitrary"),
                                             vmem_limit_bytes=VMEM_LIMIT_BYTES),
        name="hybrid_layer",
    )(x, ng, wc, wa, cqg, ckvg, wuqt, wuk, wuvt, gq, gk, cos, sin, cost, sint, km, vm, convw, convb, lng, lnb, ws, bs,
      mqg, bm, wb, wo)


def _swap_rope_halves(r):
    return jnp.concatenate([r[..., HALF_ROPE:], r[..., :HALF_ROPE]], axis=-1)


def _slab(nope, rope):
    pad = jnp.zeros(rope.shape[:-1] + (LANES - QK_HEAD,), rope.dtype)
    return jnp.concatenate([nope, rope, pad], axis=-1)


def _head_gain_slab(g):
    rope = g[..., QK_NOPE:]
    nope = g[..., :QK_NOPE]
    return jnp.stack([_slab(nope, rope), _slab(jnp.zeros_like(nope), _swap_rope_halves(rope))], axis=-2)


def _latent_weights(w_in):
    kr = w_in[:, Q_LORA + KV_LORA:OFF_CONV]
    no_nope = jnp.zeros((D_MODEL, QK_NOPE), F32)
    return jnp.concatenate([w_in[:, :Q_LORA + KV_LORA], _slab(no_nope, kr), _slab(no_nope, _swap_rope_halves(kr))],
                           axis=-1).astype(BF16)


def _prep(w_uq, w_ukv, q_g, k_g, b_spatial):
    n = w_uq.shape[0]
    uq = w_uq.reshape(n, Q_LORA, MLA_HEADS, QK_HEAD)
    wuqt = _slab(uq[..., :QK_NOPE], uq[..., QK_NOPE:]).reshape(n, Q_LORA, MLA_HEADS * LANES)
    wuqt = jnp.swapaxes(wuqt, 1, 2).astype(BF16)
    ukv = w_ukv.reshape(n, KV_LORA, MLA_HEADS, QK_NOPE + V_HEAD)
    wuk = jnp.concatenate([ukv[..., :QK_NOPE], jnp.zeros((n, KV_LORA, MLA_HEADS, LANES - QK_NOPE), F32)], axis=-1)
    wuk = wuk.reshape(n, KV_LORA, MLA_HEADS * LANES).astype(BF16)
    wuvt = jnp.swapaxes(ukv[..., QK_NOPE:].reshape(n, KV_LORA, MLA_HEADS * V_HEAD), 1, 2).astype(BF16)
    bs = jnp.repeat(jnp.swapaxes(b_spatial, 1, 2), SG_WIDTH // SG_GROUPS, axis=2)
    return wuqt, wuk, wuvt, jnp.swapaxes(_head_gain_slab(q_g), 1, 2), _head_gain_slab(k_g), bs


def kernel(x, mem, positions, norm_g, w_in, cq_norm_g, ckv_norm_g, w_uq, w_ukv, mla_q_norm_g, mla_k_norm_g,
           conv_w, conv_b, sg_ln_g, sg_ln_b, w_spatial, b_spatial, mem_norm_g, w_mem_kv, mem_q_norm_g,
           mem_k_norm_g, b_merge, w_branch, w_out):
    assert x.shape[1] % TILE_ROPE == 0 and x.shape[2] == D_MODEL
    cos, sin, cost, sint = _rope_tables(positions)
    row = lambda a: a[:, None, :]
    wuqt, wuk, wuvt, gq, gk, bs = _prep(w_uq, w_ukv, mla_q_norm_g, mla_k_norm_g, b_spatial)
    wmkv, wb, wo = w_mem_kv.astype(BF16), w_branch.astype(BF16), w_out.astype(BF16)
    for l in range(DEPTH):
        km, vm = _mem_kv(l, mem, row(mem_norm_g), wmkv, row(mem_k_norm_g))
        wa = _latent_weights(w_in[l])
        wc = w_in[l][:, OFF_CONV:].astype(BF16)
        x = _layer(l, x, row(norm_g), wc, wa, row(cq_norm_g), row(ckv_norm_g), wuqt, wuk, wuvt, gq, gk,
                   cos, sin, cost, sint, km, vm, conv_w, row(conv_b), row(sg_ln_g), row(sg_ln_b),
                   w_spatial, bs, row(mem_q_norm_g), b_merge, wb, wo)
    return x
```
